```python
import jax, jax.numpy as jnp
from jax import lax
import numpy as np

D_MODEL = 1024
BATCH = 16
SEQ = 256
DEPTH = 4
DEC_BATCH = 4
DEC_SEQ = 2048
PAST_LEN = 512

GRID_W = 64
N_MIXERS = 2
N_CONV_LAYERS = (DEPTH + 1) // 2
N_NA_LAYERS = DEPTH // 2
CONV_WIDTH = D_MODEL
CONV_K = 31
N_HEADS = 16
HEAD_DIM = D_MODEL // N_HEADS
NA_WIDTH = N_HEADS * HEAD_DIM
NA_ROW_WIN = 8
NA_COL_WIN = 16
Q_BLOCK = 128
EPS = 1e-6
NEG_INF = -1e30

kernel_name = 'hybrid_conv_natten_prefix_dit_step'


def rms_norm(x, g):
    x32 = x.astype(jnp.float32)
    y = x32 * lax.rsqrt(jnp.mean(x32 * x32, axis=-1, keepdims=True) + EPS)
    return y.astype(x.dtype) * g


def layer_norm(x, g, b):
    x32 = x.astype(jnp.float32)
    mu = jnp.mean(x32, axis=-1, keepdims=True)
    var = jnp.mean(jnp.square(x32 - mu), axis=-1, keepdims=True)
    return ((x32 - mu) * lax.rsqrt(var + EPS)).astype(x.dtype) * g + b


def ada_params(cvec, w, b):
    m = jax.nn.silu(cvec) @ w + b
    return jnp.split(m, 3, axis=-1)


def modulate(x, g, shift, scale):
    return rms_norm(x, g) * (1 + scale[:, None]) + shift[:, None]


def conv_branch(h, w_in, dw_w, dw_b, ln_g, ln_b, w_out):
    a, b, z = jnp.split(h @ w_in, 3, axis=-1)
    u = a * jax.nn.sigmoid(b)
    u = lax.conv_general_dilated(
        u, dw_w[:, None, :], window_strides=(1,),
        padding=[(CONV_K // 2, CONV_K // 2)],
        dimension_numbers=('NWC', 'WIO', 'NWC'),
        feature_group_count=CONV_WIDTH) + dw_b
    u = layer_norm(u, ln_g, ln_b)
    u = jax.nn.silu(u) * jax.nn.silu(z)
    return u @ w_out


def na_project(h, w_in):
    B, L, _ = h.shape
    q, k, v, z = jnp.split(h @ w_in, 4, axis=-1)
    heads = lambda t: t.reshape(B, L, N_HEADS, HEAD_DIM)
    return heads(q), heads(k), heads(v), z


def context_attention(q, k, v):
    B, L, H, Dh = q.shape
    nb = L // Q_BLOCK
    q = q * (HEAD_DIM ** -0.5)
    qb = q.reshape(B, nb, Q_BLOCK, H, Dh).swapaxes(0, 1)

    def attend(qi):
        s = jnp.einsum('bqhd,bkhd->bhqk', qi, k).astype(jnp.float32)
        p = jax.nn.softmax(s, axis=-1).astype(v.dtype)
        return jnp.einsum('bhqk,bkhd->bqhd', p, v)

    o = lax.map(attend, qb)
    return o.swapaxes(0, 1).reshape(B, L, H * Dh)


def neighbourhood_attention(q, k, v, k_ctx, v_ctx, rpb):
    B, N, H, Dh = q.shape
    rows = N // GRID_W
    kr = min(NA_ROW_WIN, rows)
    kc = NA_COL_WIN
    ncb = GRID_W // kc
    kc2 = 2 * kc
    blk_start = np.clip(np.arange(ncb) * kc - kc // 2, 0, GRID_W - kc2)
    key_col = blk_start[:, None] + np.arange(kc2)[None, :]
    q_col = np.arange(GRID_W).reshape(ncb, kc)
    q_start = np.clip(q_col - kc // 2, 0, GRID_W - kc)
    in_win = (key_col[:, None, :] >= q_start[..., None]) & (key_col[:, None, :] < q_start[..., None] + kc)
    dc_idx = np.clip(key_col[:, None, :] - q_col[..., None] + kc - 1, 0, 2 * kc - 2)
    mask = jnp.asarray(in_win)[:, :, None, :]

    q = q * (HEAD_DIM ** -0.5)
    k_grid = k.reshape(B, rows, GRID_W, H, Dh)
    v_grid = v.reshape(B, rows, GRID_W, H, Dh)
    q_rows = q.reshape(B, rows, ncb, kc, H, Dh).swapaxes(0, 1)
    n_loc = kr * kc2

    def row_block(args):
        r, qb = args
        rs = jnp.clip(r - kr // 2, 0, rows - kr)
        k_rows = lax.dynamic_slice_in_dim(k_grid, rs, kr, axis=1)
        v_rows = lax.dynamic_slice_in_dim(v_grid, rs, kr, axis=1)
        k_blk = k_rows[:, :, key_col]
        v_blk = v_rows[:, :, key_col]
        s_loc = jnp.einsum('bjqhd,brjkhd->bhjqrk', qb, k_blk).astype(jnp.float32)
        dr_idx = rs + jnp.arange(kr) - r + NA_ROW_WIN - 1
        bias = rpb[:, dr_idx][:, :, dc_idx]
        bias = bias.transpose(0, 2, 3, 1, 4).astype(jnp.float32)
        s_loc = jnp.where(mask, s_loc + bias, NEG_INF).reshape(B, H, ncb, kc, n_loc)
        s_ctx = jnp.einsum('bjqhd,blhd->bhjql', qb, k_ctx).astype(jnp.float32)
        p = jax.nn.softmax(jnp.concatenate([s_loc, s_ctx], axis=-1), axis=-1).astype(v.dtype)
        p_loc = p[..., :n_loc].reshape(B, H, ncb, kc, kr, kc2)
        p_ctx = p[..., n_loc:]
        o = (jnp.einsum('bhjqrk,brjkhd->bjqhd', p_loc, v_blk)
             + jnp.einsum('bhjql,blhd->bjqhd', p_ctx, v_ctx))
        return o.reshape(B, GRID_W, H * Dh)

    out = lax.map(row_block, (jnp.arange(rows), q_rows))
    return out.swapaxes(0, 1).reshape(B, N, H * Dh)


def setup_inputs(seed: int = 0) -> dict:
    key = jax.random.key(seed)
    ks = jax.random.split(key, 20)
    f32 = jnp.float32
    nrm = lambda k, shape, s: jax.random.normal(k, shape, f32) * s
    return {
        'x_prompt': nrm(ks[0], (BATCH, SEQ, D_MODEL), 1.0),
        'x_sample': nrm(ks[1], (DEC_BATCH, DEC_SEQ, D_MODEL), 1.0),
        'c': nrm(ks[2], (DEC_BATCH, D_MODEL), 1.0),
        'cache_k': nrm(ks[3], (DEC_BATCH, N_NA_LAYERS, PAST_LEN, N_HEADS, HEAD_DIM), 1.0),
        'cache_v': nrm(ks[4], (DEC_BATCH, N_NA_LAYERS, PAST_LEN, N_HEADS, HEAD_DIM), 1.0),
        'c_ctx': nrm(ks[5], (D_MODEL,), 1.0),
        'norm_g': 1.0 + nrm(ks[6], (DEPTH, D_MODEL), 0.02),
        'ada_w': nrm(ks[7], (DEPTH, D_MODEL, 3 * D_MODEL), 0.5 * D_MODEL ** -0.5),
        'ada_b': nrm(ks[8], (DEPTH, 3 * D_MODEL), 0.02),
        'conv_w_in': nrm(ks[9], (N_CONV_LAYERS, D_MODEL, 3 * CONV_WIDTH), D_MODEL ** -0.5),
        'conv_dw_w': nrm(ks[10], (N_CONV_LAYERS, CONV_K, CONV_WIDTH), CONV_K ** -0.5),
        'conv_dw_b': nrm(ks[11], (N_CONV_LAYERS, CONV_WIDTH), 0.02),
        'conv_ln_g': 1.0 + nrm(ks[12], (N_CONV_LAYERS, CONV_WIDTH), 0.02),
        'conv_ln_b': nrm(ks[13], (N_CONV_LAYERS, CONV_WIDTH), 0.02),
        'conv_w_out': nrm(ks[14], (N_CONV_LAYERS, CONV_WIDTH, D_MODEL), CONV_WIDTH ** -0.5),
        'na_w_in': nrm(ks[15], (N_NA_LAYERS, D_MODEL, 4 * NA_WIDTH), D_MODEL ** -0.5),
        'na_rpb': nrm(ks[16], (N_NA_LAYERS, N_HEADS, 2 * NA_ROW_WIN - 1, 2 * NA_COL_WIN - 1), 0.1),
        'na_w_out': nrm(ks[17], (N_NA_LAYERS, NA_WIDTH, D_MODEL), NA_WIDTH ** -0.5),
        'final_g': 1.0 + nrm(ks[18], (D_MODEL,), 0.02),
    }


def reference(x_prompt, x_sample, c, cache_k, cache_v, c_ctx, norm_g, ada_w, ada_b,
              conv_w_in, conv_dw_w, conv_dw_b, conv_ln_g, conv_ln_b, conv_w_out,
              na_w_in, na_rpb, na_w_out, final_g):
    x = x_prompt
    new_k, new_v = [], []
    for i in range(DEPTH):
        shift, scale, gate = ada_params(c_ctx[None], ada_w[i], ada_b[i])
        h = modulate(x, norm_g[i], shift, scale)
        j = i // N_MIXERS
        if i % N_MIXERS == 0:
            out = conv_branch(h, conv_w_in[j], conv_dw_w[j], conv_dw_b[j],
                              conv_ln_g[j], conv_ln_b[j], conv_w_out[j])
        else:
            q, k, v, z = na_project(h, na_w_in[j])
            new_k.append(k)
            new_v.append(v)
            out = (context_attention(q, k, v) * jax.nn.silu(z)) @ na_w_out[j]
        x = x + gate[:, None] * out
    y_prompt = rms_norm(x, final_g)
    new_cache_k = jnp.stack(new_k, axis=1)
    new_cache_v = jnp.stack(new_v, axis=1)

    x = x_sample
    for i in range(DEPTH):
        shift, scale, gate = ada_params(c, ada_w[i], ada_b[i])
        h = modulate(x, norm_g[i], shift, scale)
        j = i // N_MIXERS
        if i % N_MIXERS == 0:
            out = conv_branch(h, conv_w_in[j], conv_dw_w[j], conv_dw_b[j],
                              conv_ln_g[j], conv_ln_b[j], conv_w_out[j])
        else:
            q, k, v, z = na_project(h, na_w_in[j])
            att = neighbourhood_attention(q, k, v, cache_k[:, j], cache_v[:, j], na_rpb[j])
            out = (att * jax.nn.silu(z)) @ na_w_out[j]
        x = x + gate[:, None] * out
    y_sample = rms_norm(x, final_g)
    return (y_prompt, y_sample, new_cache_k, new_cache_v)
```

```python
import functools

import numpy as np
import jax
import jax.numpy as jnp
from jax import lax
from jax.experimental import pallas as pl
from jax.experimental.pallas import tpu as pltpu

F32 = jnp.float32
BF16 = jnp.bfloat16

D_MODEL = 1024
DEPTH = 4
N_HEADS = 16
HEAD_DIM = 64
GRID_W = 64
GRID_ROWS = 32
CONV_K = 31
CONV_HALO = 16
ROW_WIN = 8
COL_WIN = 16
EPS = 1e-6
NEG_INF = -1e30
MOD_ROWS = 16

Q_ROWS = 4
Q_BLK = Q_ROWS * GRID_W
K_ROWS = 12
K_BLK = K_ROWS * GRID_W
N_QBLK = GRID_ROWS // Q_ROWS

VMEM_LIMIT_BYTES = 56 * 1024 * 1024


def _params(sem):
    return pltpu.CompilerParams(dimension_semantics=sem, vmem_limit_bytes=VMEM_LIMIT_BYTES)


def _sigmoid(x):
    return 1.0 / (1.0 + jnp.exp(-x))


def _silu(x):
    return x * _sigmoid(x)


def _rms(x):
    return x * lax.rsqrt(jnp.mean(x * x, axis=-1, keepdims=True) + EPS)


def _dot(a, b):
    return jnp.dot(a, b, preferred_element_type=F32)


def _dot_nt(a, b):
    return lax.dot_general(a, b, (((1,), (1,)), ((), ())), preferred_element_type=F32)


ADA_TILE = 768


def _ada_body(c_ref, w_ref, b_ref, o_ref):
    s = _silu(c_ref[...]).astype(BF16)
    o_ref[...] = _dot(s, w_ref[...].astype(BF16)) + b_ref[...]


def _ada(cvec, ada_w, ada_b):
    n_out = 3 * D_MODEL
    return pl.pallas_call(
        _ada_body,
        grid=(DEPTH, n_out // ADA_TILE),
        in_specs=[pl.BlockSpec((MOD_ROWS, D_MODEL), lambda l, n: (0, 0)),
                  pl.BlockSpec((None, D_MODEL, ADA_TILE), lambda l, n: (l, 0, n)),
                  pl.BlockSpec((None, 1, ADA_TILE), lambda l, n: (l, 0, n))],
        out_specs=pl.BlockSpec((None, MOD_ROWS, ADA_TILE), lambda l, n: (l, 0, n)),
        out_shape=jax.ShapeDtypeStruct((DEPTH, MOD_ROWS, n_out), F32),
        compiler_params=_params(("arbitrary", "arbitrary")),
        name="ada_params",
    )(cvec, ada_w, ada_b.reshape(DEPTH, 1, n_out))


def _modulate(x, gs, shift):
    return (_rms(x) * gs + shift).astype(BF16)


PRE_CHUNK = 64
CONV_CHUNK = 32
COL_CHUNK = 512


def _conv_body(*refs, tile, halo, tiles_per_seq):
    if halo:
        (x_ref, xp_ref, xn_ref, mod_ref, g_ref, win_ref, dww_ref, dwb_ref, lng_ref, lnb_ref, wout_ref,
         o_ref, h_s, u_s, sz_s, gt_s) = refs
    else:
        (x_ref, mod_ref, g_ref, win_ref, dww_ref, dwb_ref, lng_ref, lnb_ref, wout_ref,
         o_ref, h_s, u_s, sz_s, gt_s) = refs
    d = D_MODEL
    shift = mod_ref[0:1, :]
    gate = mod_ref[2:3, :]
    gs = g_ref[...] * (1.0 + mod_ref[1:2, :])
    h_off = CONV_HALO if halo else 0
    m_rows = tile + 2 * CONV_HALO if halo else tile
    u_off = 0 if halo else CONV_HALO

    def pre(c, carry):
        r0 = pl.multiple_of(c * PRE_CHUNK, PRE_CHUNK)
        h_s[pl.ds(h_off + r0, PRE_CHUNK), :] = _modulate(x_ref[pl.ds(r0, PRE_CHUNK), :], gs, shift)
        return carry

    lax.fori_loop(0, tile // PRE_CHUNK, pre, 0)
    if halo:
        h_s[0:CONV_HALO, :] = _modulate(xp_ref[...], gs, shift)
        h_s[tile + CONV_HALO:tile + 2 * CONV_HALO, :] = _modulate(xn_ref[...], gs, shift)

    for cb in range(d // COL_CHUNK):
        cs = cb * COL_CHUNK
        a = _dot(h_s[...], win_ref[:, cs:cs + COL_CHUNK])
        b = _dot(h_s[...], win_ref[:, d + cs:d + cs + COL_CHUNK])
        u_s[u_off:u_off + m_rows, cs:cs + COL_CHUNK] = a * _sigmoid(b)
        z = _dot(h_s[h_off:h_off + tile, :], win_ref[:, 2 * d + cs:2 * d + cs + COL_CHUNK])
        sz_s[:, cs:cs + COL_CHUNK] = _silu(z)

    zeros = jnp.zeros((CONV_HALO, d), F32)
    if halo:
        i = pl.program_id(0)

        @pl.when(i % tiles_per_seq == 0)
        def _():
            u_s[0:CONV_HALO, :] = zeros

        @pl.when(i % tiles_per_seq == tiles_per_seq - 1)
        def _():
            u_s[tile + CONV_HALO:tile + 2 * CONV_HALO, :] = zeros
    else:
        u_s[0:CONV_HALO, :] = zeros
        u_s[tile + CONV_HALO:tile + 2 * CONV_HALO, :] = zeros

    def conv(c):
        r0 = c * CONV_CHUNK
        acc = jnp.broadcast_to(dwb_ref[...], (CONV_CHUNK, d))
        for k in range(CONV_K):
            acc = acc + u_s[pl.ds(r0 + (k + CONV_HALO - CONV_K // 2), CONV_CHUNK), :] * dww_ref[k:k + 1, :]
        mu = jnp.mean(acc, axis=-1, keepdims=True)
        dev = acc - mu
        var = jnp.mean(dev * dev, axis=-1, keepdims=True)
        y = dev * lax.rsqrt(var + EPS) * lng_ref[...] + lnb_ref[...]
        gt_s[pl.ds(r0, CONV_CHUNK), :] = (_silu(y) * sz_s[pl.ds(r0, CONV_CHUNK), :]).astype(BF16)

    for c in range(tile // CONV_CHUNK):
        conv(c)
    o_ref[...] = x_ref[...] + gate * _dot(gt_s[...], wout_ref[...])


def _mod_spec(row0, row_step, tiles_per_seq):
    return pl.BlockSpec((None, 3, D_MODEL), lambda i: (row0 + row_step * (i // tiles_per_seq), 0, 0))


def _conv_layer(x, mod, row0, row_step, seq_len, tile, g, w_in, dw_w, dw_b, ln_g, ln_b, w_out):
    n_tok, d = x.shape
    tiles_per_seq = seq_len // tile
    halo = tiles_per_seq > 1
    hb = tile // CONV_HALO
    n_hblk = n_tok // CONV_HALO
    const = lambda i: (0, 0)
    in_specs = [pl.BlockSpec((tile, d), lambda i: (i, 0))]
    args = [x]
    if halo:
        in_specs += [pl.BlockSpec((CONV_HALO, d), lambda i: (jnp.maximum(i * hb - 1, 0), 0)),
                     pl.BlockSpec((CONV_HALO, d), lambda i: (jnp.minimum((i + 1) * hb, n_hblk - 1), 0))]
        args += [x, x]
    in_specs += [_mod_spec(row0, row_step, tiles_per_seq),
                 pl.BlockSpec((1, d), const),
                 pl.BlockSpec((d, 3 * d), const),
                 pl.BlockSpec((CONV_K, d), const),
                 pl.BlockSpec((1, d), const),
                 pl.BlockSpec((1, d), const),
                 pl.BlockSpec((1, d), const),
                 pl.BlockSpec((d, d), const)]
    args += [mod, g, w_in, dw_w, dw_b, ln_g, ln_b, w_out]
    m_rows = tile + 2 * CONV_HALO if halo else tile
    return pl.pallas_call(
        functools.partial(_conv_body, tile=tile, halo=halo, tiles_per_seq=tiles_per_seq),
        grid=(n_tok // tile,),
        in_specs=in_specs,
        out_specs=pl.BlockSpec((tile, d), lambda i: (i, 0)),
        out_shape=jax.ShapeDtypeStruct((n_tok, d), F32),
        scratch_shapes=[pltpu.VMEM((m_rows, d), BF16),
                        pltpu.VMEM((tile + 2 * CONV_HALO, d), F32),
                        pltpu.VMEM((tile, d), F32),
                        pltpu.VMEM((tile, d), BF16)],
        compiler_params=_params(("arbitrary",)),
        name="conv_layer",
    )(*args)


def _head_masks():
    lane = lax.broadcasted_iota(jnp.int32, (1, 2 * HEAD_DIM), 1)
    first = lane < HEAD_DIM
    return first, jnp.logical_not(first)


def _na_ctx_body(*refs, final, n_extra):
    x_ref, mod_ref, g_ref, win_ref, wout_ref = refs[:5]
    fg_ref = refs[5] if final else None
    o_ref, k_ref, v_ref, q_s, kb_s, vb_s, sz_s, att_s = refs[5 + n_extra:]
    d = D_MODEL
    x = x_ref[...]
    gs = g_ref[...] * (1.0 + mod_ref[1:2, :])
    h = _modulate(x, gs, mod_ref[0:1, :])
    for cb in range(d // COL_CHUNK):
        cs = cb * COL_CHUNK
        q_s[:, cs:cs + COL_CHUNK] = (_dot(h, win_ref[:, cs:cs + COL_CHUNK]) * (HEAD_DIM ** -0.5)).astype(BF16)
        k = _dot(h, win_ref[:, d + cs:d + cs + COL_CHUNK])
        k_ref[:, cs:cs + COL_CHUNK] = k
        kb_s[:, cs:cs + COL_CHUNK] = k.astype(BF16)
        v = _dot(h, win_ref[:, 2 * d + cs:2 * d + cs + COL_CHUNK])
        v_ref[:, cs:cs + COL_CHUNK] = v
        vb_s[:, cs:cs + COL_CHUNK] = v.astype(BF16)
        sz_s[:, cs:cs + COL_CHUNK] = _silu(_dot(h, win_ref[:, 3 * d + cs:3 * d + cs + COL_CHUNK]))

    masks = _head_masks()
    for hp in range(N_HEADS // 2):
        ls = hp * 2 * HEAD_DIM
        q = q_s[:, ls:ls + 2 * HEAD_DIM]
        kp = kb_s[:, ls:ls + 2 * HEAD_DIM]
        vp = vb_s[:, ls:ls + 2 * HEAD_DIM]
        outs = []
        for hh in range(2):
            s = _dot_nt(jnp.where(masks[hh], q, jnp.zeros_like(q)), kp)
            p = jnp.exp(s - jnp.max(s, axis=-1, keepdims=True))
            l = jnp.sum(p, axis=-1, keepdims=True)
            outs.append(_dot(p.astype(BF16), vp) / l)
        att = jnp.where(masks[0], outs[0], outs[1])
        att_s[:, ls:ls + 2 * HEAD_DIM] = (att * sz_s[:, ls:ls + 2 * HEAD_DIM]).astype(BF16)

    y = x + mod_ref[2:3, :] * _dot(att_s[...], wout_ref[...])
    if final:
        y = _rms(y) * fg_ref[...]
    o_ref[...] = y


def _na_ctx_layer(x, mod, g, w_in, w_out, layer_j, n_layers, seq_len, final_g=None, caches=None):
    n_tok, d = x.shape
    n_seq = n_tok // seq_len
    const = lambda b: (0, 0)
    final = final_g is not None
    in_specs = [pl.BlockSpec((seq_len, d), lambda b: (b, 0)),
                pl.BlockSpec((None, 3, d), lambda b: (0, 0, 0)),
                pl.BlockSpec((1, d), const),
                pl.BlockSpec((d, 4 * d), const),
                pl.BlockSpec((d, d), const)]
    args = [x, mod, g, w_in, w_out]
    aliases = {}
    if final:
        in_specs.append(pl.BlockSpec((1, d), const))
        args.append(final_g)
    if caches is not None:
        aliases = {len(args): 1, len(args) + 1: 2}
        in_specs += [pl.BlockSpec(memory_space=pl.ANY)] * 2
        args += list(caches)
    cache_shape = jax.ShapeDtypeStruct((n_seq, n_layers, seq_len, d), F32)
    cache_spec = pl.BlockSpec((None, None, seq_len, d), lambda b: (b, layer_j, 0, 0))
    return pl.pallas_call(
        functools.partial(_na_ctx_body, final=final, n_extra=len(args) - 5),
        grid=(n_seq,),
        in_specs=in_specs,
        out_specs=[pl.BlockSpec((seq_len, d), lambda b: (b, 0)), cache_spec, cache_spec],
        out_shape=[jax.ShapeDtypeStruct((n_tok, d), F32), cache_shape, cache_shape],
        scratch_shapes=[pltpu.VMEM((seq_len, d), BF16),
                        pltpu.VMEM((seq_len, d), BF16),
                        pltpu.VMEM((seq_len, d), BF16),
                        pltpu.VMEM((seq_len, d), F32),
                        pltpu.VMEM((seq_len, d), BF16)],
        input_output_aliases=aliases,
        compiler_params=_params(("arbitrary",)),
        name="na_ctx_layer",
    )(*args)


def _na_proj_body(x_ref, mod_ref, g_ref, win_ref, q_ref, k_ref, v_ref, sz_ref):
    d = D_MODEL
    gs = g_ref[...] * (1.0 + mod_ref[1:2, :])
    h = _modulate(x_ref[...], gs, mod_ref[0:1, :])
    for cb in range(d // COL_CHUNK):
        cs = cb * COL_CHUNK
        q_ref[:, cs:cs + COL_CHUNK] = (_dot(h, win_ref[:, cs:cs + COL_CHUNK]) * (HEAD_DIM ** -0.5)).astype(BF16)
        k_ref[:, cs:cs + COL_CHUNK] = _dot(h, win_ref[:, d + cs:d + cs + COL_CHUNK]).astype(BF16)
        v_ref[:, cs:cs + COL_CHUNK] = _dot(h, win_ref[:, 2 * d + cs:2 * d + cs + COL_CHUNK]).astype(BF16)
        sz_ref[:, cs:cs + COL_CHUNK] = _silu(_dot(h, win_ref[:, 3 * d + cs:3 * d + cs + COL_CHUNK])).astype(BF16)


def _na_proj(x, mod, row0, seq_len, tile, g, w_in):
    n_tok, d = x.shape
    tiles_per_seq = seq_len // tile
    const = lambda i: (0, 0)
    tok_spec = pl.BlockSpec((tile, d), lambda i: (i, 0))
    out = jax.ShapeDtypeStruct((n_tok, d), BF16)
    return pl.pallas_call(
        _na_proj_body,
        grid=(n_tok // tile,),
        in_specs=[tok_spec,
                  _mod_spec(row0, 1, tiles_per_seq),
                  pl.BlockSpec((1, d), const),
                  pl.BlockSpec((d, 4 * d), const)],
        out_specs=[tok_spec] * 4,
        out_shape=[out] * 4,
        compiler_params=_params(("arbitrary",)),
        name="na_proj",
    )(x, mod, g, w_in)


def _block_rows(qb):
    ks = min(max(Q_ROWS * qb - ROW_WIN // 2, 0), GRID_ROWS - K_ROWS)
    rs = [min(max(Q_ROWS * qb + qr - ROW_WIN // 2, 0), GRID_ROWS - ROW_WIN) for qr in range(Q_ROWS)]
    return ks, rs


_CLASS_BLOCKS = (0, 2, N_QBLK - 1)


def _na_att_body(q_ref, k_ref, v_ref, kc_ref, vc_ref, t2_ref, o_ref, bias_s, kc_s, vc_s):
    masks = _head_masks()
    first = masks[0]
    kc_s[...] = kc_ref[...].astype(BF16)
    vc_s[...] = vc_ref[...].astype(BF16)

    neg = jnp.full((GRID_W, 2 * GRID_W), NEG_INF, F32)
    for cls, qb in enumerate(_CLASS_BLOCKS):
        ks, rs = _block_rows(qb)
        for hh in range(2):
            for qr in range(Q_ROWS):
                r = Q_ROWS * qb + qr
                for m in range(K_ROWS // 2):
                    ka = ks + 2 * m
                    va = rs[qr] <= ka < rs[qr] + ROW_WIN
                    vb = rs[qr] <= ka + 1 < rs[qr] + ROW_WIN
                    if va or vb:
                        t = t2_ref[hh, ka - r + (ROW_WIN - 1) + 1]
                        if not va:
                            t = jnp.where(first, neg, t)
                        if not vb:
                            t = jnp.where(first, t, neg)
                    else:
                        t = neg
                    bias_s[cls * 2 + hh, qr * GRID_W:(qr + 1) * GRID_W, m * 2 * GRID_W:(m + 1) * 2 * GRID_W] = t

    def block(q0, k0, cls):
        q = q_ref[pl.ds(q0, Q_BLK), :]
        kl = k_ref[pl.ds(k0, K_BLK), :]
        vl = v_ref[pl.ds(k0, K_BLK), :]
        outs = []
        for hh in range(2):
            qm = jnp.where(masks[hh], q, jnp.zeros_like(q))
            sl = _dot_nt(qm, kl) + bias_s[cls * 2 + hh]
            sc = _dot_nt(qm, kc_s[...])
            mx = jnp.maximum(jnp.max(sl, axis=-1, keepdims=True), jnp.max(sc, axis=-1, keepdims=True))
            p_l = jnp.exp(sl - mx)
            p_c = jnp.exp(sc - mx)
            l = jnp.sum(p_l, axis=-1, keepdims=True) + jnp.sum(p_c, axis=-1, keepdims=True)
            outs.append((_dot(p_l.astype(BF16), vl) + _dot(p_c.astype(BF16), vc_s[...])) / l)
        o_ref[pl.ds(q0, Q_BLK), :] = jnp.where(first, outs[0], outs[1]).astype(BF16)

    block(0, _block_rows(0)[0] * GRID_W, 0)

    def interior(qb, carry):
        q0 = pl.multiple_of(qb * Q_BLK, Q_BLK)
        block(q0, pl.multiple_of(q0 - (ROW_WIN // 2) * GRID_W, Q_BLK), 1)
        return carry

    lax.fori_loop(1, N_QBLK - 1, interior, 0)
    block((N_QBLK - 1) * Q_BLK, _block_rows(N_QBLK - 1)[0] * GRID_W, 2)


def _na_att(q, k, v, cache_k, cache_v, t2, layer_j, n_batch):
    n_tok, d = q.shape
    seq_len = n_tok // n_batch
    past = cache_k.shape[2]
    pw = 2 * HEAD_DIM
    tok_spec = pl.BlockSpec((seq_len, pw), lambda hp, b: (b, hp))
    cache_spec = pl.BlockSpec((None, None, past, pw), lambda hp, b: (b, layer_j, 0, hp))
    return pl.pallas_call(
        _na_att_body,
        grid=(N_HEADS // 2, n_batch),
        in_specs=[tok_spec, tok_spec, tok_spec, cache_spec, cache_spec,
                  pl.BlockSpec((2, 2 * ROW_WIN, GRID_W, pw), lambda hp, b: (hp, 0, 0, 0))],
        out_specs=tok_spec,
        out_shape=jax.ShapeDtypeStruct((n_tok, d), BF16),
        scratch_shapes=[pltpu.VMEM((3 * 2, Q_BLK, K_BLK), F32),
                        pltpu.VMEM((past, pw), BF16),
                        pltpu.VMEM((past, pw), BF16)],
        compiler_params=_params(("arbitrary", "arbitrary")),
        name="na_attention",
    )(q, k, v, cache_k, cache_v, t2)


def _na_out_body(*refs, final):
    if final:
        x_ref, att_ref, sz_ref, mod_ref, wout_ref, fg_ref, o_ref = refs
    else:
        x_ref, att_ref, sz_ref, mod_ref, wout_ref, o_ref = refs
    a = (att_ref[...].astype(F32) * sz_ref[...].astype(F32)).astype(BF16)
    y = x_ref[...] + mod_ref[2:3, :] * _dot(a, wout_ref[...])
    if final:
        y = _rms(y) * fg_ref[...]
    o_ref[...] = y


def _na_out(x, att, sz, mod, row0, seq_len, tile, w_out, final_g=None):
    n_tok, d = x.shape
    tiles_per_seq = seq_len // tile
    const = lambda i: (0, 0)
    tok_spec = pl.BlockSpec((tile, d), lambda i: (i, 0))
    final = final_g is not None
    in_specs = [tok_spec, tok_spec, tok_spec,
                _mod_spec(row0, 1, tiles_per_seq),
                pl.BlockSpec((d, d), const)]
    args = [x, att, sz, mod, w_out]
    if final:
        in_specs.append(pl.BlockSpec((1, d), const))
        args.append(final_g)
    return pl.pallas_call(
        functools.partial(_na_out_body, final=final),
        grid=(n_tok // tile,),
        in_specs=in_specs,
        out_specs=tok_spec,
        out_shape=jax.ShapeDtypeStruct((n_tok, d), F32),
        compiler_params=_params(("arbitrary",)),
        name="na_out",
    )(*args)


def _bias_tables(rpb):
    qc = np.arange(GRID_W)[:, None]
    kc = np.arange(GRID_W)[None, :]
    start = np.clip(qc - COL_WIN // 2, 0, GRID_W - COL_WIN)
    in_win = (kc >= start) & (kc < start + COL_WIN)
    dc = np.clip(kc - qc + COL_WIN - 1, 0, 2 * COL_WIN - 2)
    c = jnp.where(jnp.asarray(in_win), rpb[:, :, dc], NEG_INF)
    neg = jnp.full((rpb.shape[0], 1, GRID_W, GRID_W), NEG_INF, F32)
    cx = jnp.concatenate([neg, c, neg], axis=1)
    return jnp.concatenate([cx[:, :2 * ROW_WIN], cx[:, 1:2 * ROW_WIN + 1]], axis=-1)


CONV_TILE = 512
NA_TILE = 512


def kernel(x_prompt, x_sample, c, cache_k, cache_v, c_ctx, norm_g, ada_w, ada_b, conv_w_in, conv_dw_w,
           conv_dw_b, conv_ln_g, conv_ln_b, conv_w_out, na_w_in, na_rpb, na_w_out, final_g):
    n_ctx, ctx_len, d = x_prompt.shape
    n_dec, dec_len, _ = x_sample.shape
    n_na = na_w_in.shape[0]
    past = cache_k.shape[2]

    cvec = jnp.concatenate([c_ctx[None], c, jnp.zeros((MOD_ROWS - 1 - n_dec, d), F32)], axis=0)
    mod = _ada(cvec, ada_w, ada_b).reshape(DEPTH, MOD_ROWS, 3, d)

    xc = x_prompt.reshape(n_ctx * ctx_len, d)
    xs = x_sample.reshape(n_dec * dec_len, d)
    ck = cache_k.reshape(n_dec, n_na, past, d)
    cv = cache_v.reshape(n_dec, n_na, past, d)
    fg = final_g.reshape(1, d)
    caches = None
    for i in range(DEPTH):
        j = i // 2
        g = norm_g[i].reshape(1, d)
        if i % 2 == 0:
            wi = conv_w_in[j].astype(BF16)
            wo = conv_w_out[j].astype(BF16)
            rest = (g, wi, conv_dw_w[j], conv_dw_b[j].reshape(1, d), conv_ln_g[j].reshape(1, d),
                    conv_ln_b[j].reshape(1, d), wo)
            xc = _conv_layer(xc, mod[i], 0, 0, ctx_len, ctx_len, *rest)
            xs = _conv_layer(xs, mod[i], 1, 1, dec_len, CONV_TILE, *rest)
        else:
            wi = na_w_in[j].astype(BF16)
            wo = na_w_out[j].astype(BF16)
            last = i == DEPTH - 1
            xc, new_k, new_v = _na_ctx_layer(xc, mod[i], g, wi, wo, j, n_na, ctx_len,
                                             final_g=fg if last else None, caches=caches)
            caches = (new_k, new_v)
            q, k, v, sz = _na_proj(xs, mod[i], 1, dec_len, NA_TILE, g, wi)
            att = _na_att(q, k, v, ck, cv, _bias_tables(na_rpb[j]), j, n_dec)
            xs = _na_out(xs, att, sz, mod[i], 1, dec_len, NA_TILE, wo, final_g=fg if last else None)

    y_prompt = xc.reshape(n_ctx, ctx_len, d)
    y_sample = xs.reshape(n_dec, dec_len, d)
    new_cache_k = caches[0].reshape(n_ctx, n_na, ctx_len, N_HEADS, HEAD_DIM)
    new_cache_v = caches[1].reshape(n_ctx, n_na, ctx_len, N_HEADS, HEAD_DIM)
    return (y_prompt, y_sample, new_cache_k, new_cache_v)
```

```python
import functools

import numpy as np
import jax
import jax.numpy as jnp
from jax import lax
from jax.experimental import pallas as pl
from jax.experimental.pallas import tpu as pltpu

F32 = jnp.float32
BF16 = jnp.bfloat16

D_MODEL = 1024
LANES = 128
N_COLB = D_MODEL // LANES
DEPTH = 4
N_HEADS = 16
HEAD_DIM = 64
GRID_W = 64
GRID_ROWS = 32
CONV_K = 31
CONV_HALO = 16
ROW_WIN = 8
COL_WIN = 16
EPS = 1e-6
NEG_INF = -1e30
MOD_ROWS = 16

Q_ROWS = 4
Q_BLK = Q_ROWS * GRID_W
K_ROWS = 12
K_BLK = K_ROWS * GRID_W
N_QBLK = GRID_ROWS // Q_ROWS

VMEM_LIMIT_BYTES = 56 * 1024 * 1024


def _params(sem):
    return pltpu.CompilerParams(dimension_semantics=sem, vmem_limit_bytes=VMEM_LIMIT_BYTES)


def _sigmoid(x):
    return 1.0 / (1.0 + jnp.exp(-x))


def _silu(x):
    return x * _sigmoid(x)


def _rms(x):
    return x * lax.rsqrt(jnp.mean(x * x, axis=-1, keepdims=True) + EPS)


def _dot(a, b):
    return jnp.dot(a, b, preferred_element_type=F32)


def _dot_nt(a, b):
    return lax.dot_general(a, b, (((1,), (1,)), ((), ())), preferred_element_type=F32)


ADA_TILE = 768


def _ada_body(c_ref, w_ref, b_ref, o_ref):
    s = _silu(c_ref[...]).astype(BF16)
    o_ref[...] = _dot(s, w_ref[...].astype(BF16)) + b_ref[...]


def _ada(cvec, ada_w, ada_b):
    n_out = 3 * D_MODEL
    return pl.pallas_call(
        _ada_body,
        grid=(DEPTH, n_out // ADA_TILE),
        in_specs=[pl.BlockSpec((MOD_ROWS, D_MODEL), lambda l, n: (0, 0)),
                  pl.BlockSpec((None, D_MODEL, ADA_TILE), lambda l, n: (l, 0, n)),
                  pl.BlockSpec((None, 1, ADA_TILE), lambda l, n: (l, 0, n))],
        out_specs=pl.BlockSpec((None, MOD_ROWS, ADA_TILE), lambda l, n: (l, 0, n)),
        out_shape=jax.ShapeDtypeStruct((DEPTH, MOD_ROWS, n_out), F32),
        compiler_params=_params(("arbitrary", "arbitrary")),
        name="ada_params",
    )(cvec, ada_w, ada_b.reshape(DEPTH, 1, n_out))


def _modulate(x, gs, shift):
    return (_rms(x) * gs + shift).astype(BF16)


def _colblock_spec(rows):
    return pl.BlockSpec((N_COLB, rows, LANES), lambda i: (0, i, 0))


def _load_colblocks(x_ref):
    return jnp.concatenate([x_ref[cb] for cb in range(N_COLB)], axis=-1)


N_SEG = 8
PRE_ROWS = 16
CONV_CHUNK = 32
COL_CHUNK = 512


def _conv_body(*refs, tile, ext_halo, tiles_per_seq, segs_per_seq):
    x_refs, refs = refs[:N_COLB], refs[N_COLB:]
    if ext_halo:
        (xp_ref, xn_ref, mod_ref, g_ref, win_ref, dww_ref, dwb_ref, lng_ref, lnb_ref, wout_ref,
         o_ref, h_s, u_s, sz_s, gt_s) = refs
    else:
        (mod_ref, g_ref, win_ref, dww_ref, dwb_ref, lng_ref, lnb_ref, wout_ref,
         o_ref, h_s, u_s, sz_s, gt_s) = refs
    d = D_MODEL
    seg = tile // N_SEG
    halo_rows = CONV_HALO * N_SEG
    main0 = halo_rows
    back0 = main0 + tile
    shift = mod_ref[0:1, :]
    gate = mod_ref[2:3, :]
    gs = g_ref[...] * (1.0 + mod_ref[1:2, :])

    def seg_rows(a):
        return pl.ds(a, N_SEG, stride=seg)

    def load_x(a):
        return jnp.concatenate([xr[seg_rows(a), :] for xr in x_refs], axis=-1)

    def pre(c, carry):
        a = c * (PRE_ROWS // N_SEG)
        xs = jnp.concatenate([load_x(a + j) for j in range(PRE_ROWS // N_SEG)], axis=0)
        h_s[pl.ds(pl.multiple_of(c * PRE_ROWS, PRE_ROWS), PRE_ROWS), :] = _modulate(xs, gs, shift)
        return carry

    lax.fori_loop(0, tile // PRE_ROWS, pre, 0, unroll=2)
    m_rows = tile
    if ext_halo:
        h_s[tile:tile + CONV_HALO, :] = _modulate(xp_ref[...], gs, shift)
        h_s[tile + CONV_HALO:tile + 2 * CONV_HALO, :] = _modulate(xn_ref[...], gs, shift)
        m_rows = tile + 2 * CONV_HALO

    for cb in range(d // COL_CHUNK):
        cs = cb * COL_CHUNK
        a = _dot(h_s[...], win_ref[:, cs:cs + COL_CHUNK])
        b = _dot(h_s[...], win_ref[:, d + cs:d + cs + COL_CHUNK])
        u_s[main0:main0 + m_rows, cs:cs + COL_CHUNK] = a * _sigmoid(b)
        z = _dot(h_s[0:tile, :], win_ref[:, 2 * d + cs:2 * d + cs + COL_CHUNK])
        sz_s[:, cs:cs + COL_CHUNK] = _silu(z)

    sub = lax.broadcasted_iota(jnp.int32, (N_SEG, 1), 0)
    if ext_halo:
        i = pl.program_id(0)
        u_prev = jnp.where(i % tiles_per_seq == 0, 0.0, u_s[back0:back0 + CONV_HALO, :])
        u_next = jnp.where(i % tiles_per_seq == tiles_per_seq - 1, 0.0,
                           u_s[back0 + CONV_HALO:back0 + 2 * CONV_HALO, :])
        starts_seq = sub == 0
        ends_seq = sub == N_SEG - 1
    else:
        starts_seq = sub % segs_per_seq == 0
        ends_seq = sub % segs_per_seq == segs_per_seq - 1
    front = u_s[back0 - halo_rows - 1:back0 - 1, :]
    back = u_s[main0 + 1:main0 + 1 + halo_rows, :]
    for j in range(CONV_HALO):
        rows = slice(j * N_SEG, (j + 1) * N_SEG)
        edge_f = jnp.broadcast_to(u_prev[j:j + 1, :], (N_SEG, d)) if ext_halo else 0.0
        edge_b = jnp.broadcast_to(u_next[j:j + 1, :], (N_SEG, d)) if ext_halo else 0.0
        u_s[j * N_SEG:(j + 1) * N_SEG, :] = jnp.where(starts_seq, edge_f, front[rows])
        u_s[back0 + j * N_SEG:back0 + (j + 1) * N_SEG, :] = jnp.where(ends_seq, edge_b, back[rows])

    def conv(c, carry):
        r0 = pl.multiple_of(c * CONV_CHUNK, CONV_CHUNK)
        cols = []
        for cb in range(N_COLB):
            lanes = slice(cb * LANES, (cb + 1) * LANES)
            acc = jnp.broadcast_to(dwb_ref[:, lanes], (CONV_CHUNK, LANES))
            for k in range(CONV_K):
                rows = pl.ds(r0 + (k + CONV_HALO - CONV_K // 2) * N_SEG, CONV_CHUNK)
                acc = acc + u_s[rows, lanes] * dww_ref[k:k + 1, lanes]
            cols.append(acc)
        acc = jnp.concatenate(cols, axis=-1)
        mu = jnp.mean(acc, axis=-1, keepdims=True)
        dev = acc - mu
        var = jnp.mean(dev * dev, axis=-1, keepdims=True)
        y = dev * lax.rsqrt(var + EPS) * lng_ref[...] + lnb_ref[...]
        gt_s[pl.ds(r0, CONV_CHUNK), :] = (_silu(y) * sz_s[pl.ds(r0, CONV_CHUNK), :]).astype(BF16)
        return carry

    lax.fori_loop(0, tile // CONV_CHUNK, conv, 0)

    u_s[0:tile, :] = gate * _dot(gt_s[...], wout_ref[...])

    def post(a, carry):
        rows = seg_rows(a)
        y = u_s[pl.ds(pl.multiple_of(a * N_SEG, N_SEG), N_SEG), :]
        for cb, xr in enumerate(x_refs):
            o_ref[cb, rows, :] = xr[rows, :] + y[:, cb * LANES:(cb + 1) * LANES]
        return carry

    lax.fori_loop(0, seg, post, 0, unroll=4)


def _mod_spec(row0, row_step, tiles_per_seq):
    return pl.BlockSpec((None, 3, D_MODEL), lambda i: (row0 + row_step * (i // tiles_per_seq), 0, 0))


def _resident(shape):
    return pl.BlockSpec(shape, lambda *_: (0,) * len(shape), pipeline_mode=pl.Buffered(1))


def _conv_layer(x, mod, row0, row_step, seq_len, tile, g, w_in, dw_w, dw_b, ln_g, ln_b, w_out):
    n_tok, d = x.shape
    seg = tile // N_SEG
    segs_per_seq = seq_len // seg
    ext_halo = segs_per_seq > N_SEG
    tiles_per_seq = max(segs_per_seq // N_SEG, 1)
    assert (segs_per_seq % N_SEG == 0) if ext_halo else (N_SEG % segs_per_seq == 0)
    hb = tile // CONV_HALO
    n_hblk = n_tok // CONV_HALO
    in_specs = [pl.BlockSpec((tile, LANES), lambda i, cb=cb: (i, cb)) for cb in range(N_COLB)]
    args = [x] * N_COLB
    if ext_halo:
        in_specs += [pl.BlockSpec((CONV_HALO, d), lambda i: (jnp.maximum(i * hb - 1, 0), 0)),
                     pl.BlockSpec((CONV_HALO, d), lambda i: (jnp.minimum((i + 1) * hb, n_hblk - 1), 0))]
        args += [x, x]
    in_specs += [_mod_spec(row0, row_step, tiles_per_seq),
                 _resident((1, d)), _resident((d, 3 * d)), _resident((CONV_K, d)),
                 _resident((1, d)), _resident((1, d)), _resident((1, d)), _resident((d, d))]
    args += [mod, g, w_in, dw_w, dw_b, ln_g, ln_b, w_out]
    h_rows = tile + 2 * CONV_HALO if ext_halo else tile
    return pl.pallas_call(
        functools.partial(_conv_body, tile=tile, ext_halo=ext_halo, tiles_per_seq=tiles_per_seq,
                          segs_per_seq=segs_per_seq),
        grid=(n_tok // tile,),
        in_specs=in_specs,
        out_specs=pl.BlockSpec((N_COLB, tile, LANES), lambda i: (0, i, 0)),
        out_shape=jax.ShapeDtypeStruct((N_COLB, n_tok, LANES), F32),
        scratch_shapes=[pltpu.VMEM((h_rows, d), BF16),
                        pltpu.VMEM((tile + 2 * CONV_HALO * N_SEG, d), F32),
                        pltpu.VMEM((tile, d), F32),
                        pltpu.VMEM((tile, d), BF16)],
        compiler_params=_params(("arbitrary",)),
        name="conv_layer",
    )(*args)


def _head_masks():
    lane = lax.broadcasted_iota(jnp.int32, (1, 2 * HEAD_DIM), 1)
    first = lane < HEAD_DIM
    return first, jnp.logical_not(first)


def _na_ctx_body(*refs, final, n_extra):
    x_ref, mod_ref, g_ref, win_ref, wout_ref = refs[:5]
    fg_ref = refs[5] if final else None
    o_ref, k_ref, v_ref, q_s, kb_s, vb_s, sz_s, att_s = refs[5 + n_extra:]
    d = D_MODEL
    x = _load_colblocks(x_ref)
    gs = g_ref[...] * (1.0 + mod_ref[1:2, :])
    h = _modulate(x, gs, mod_ref[0:1, :])
    for cb in range(d // COL_CHUNK):
        cs = cb * COL_CHUNK
        q_s[:, cs:cs + COL_CHUNK] = (_dot(h, win_ref[:, cs:cs + COL_CHUNK]) * (HEAD_DIM ** -0.5)).astype(BF16)
        k = _dot(h, win_ref[:, d + cs:d + cs + COL_CHUNK])
        k_ref[:, cs:cs + COL_CHUNK] = k
        kb_s[:, cs:cs + COL_CHUNK] = k.astype(BF16)
        v = _dot(h, win_ref[:, 2 * d + cs:2 * d + cs + COL_CHUNK])
        v_ref[:, cs:cs + COL_CHUNK] = v
        vb_s[:, cs:cs + COL_CHUNK] = v.astype(BF16)
        sz_s[:, cs:cs + COL_CHUNK] = _silu(_dot(h, win_ref[:, 3 * d + cs:3 * d + cs + COL_CHUNK]))

    masks = _head_masks()
    for hp in range(N_HEADS // 2):
        ls = hp * 2 * HEAD_DIM
        q = q_s[:, ls:ls + 2 * HEAD_DIM]
        kp = kb_s[:, ls:ls + 2 * HEAD_DIM]
        vp = vb_s[:, ls:ls + 2 * HEAD_DIM]
        outs = []
        for hh in range(2):
            s = _dot_nt(jnp.where(masks[hh], q, jnp.zeros_like(q)), kp)
            p = jnp.exp(s - jnp.max(s, axis=-1, keepdims=True))
            l = jnp.sum(p, axis=-1, keepdims=True)
            outs.append(_dot(p.astype(BF16), vp) / l)
        att = jnp.where(masks[0], outs[0], outs[1])
        att_s[:, ls:ls + 2 * HEAD_DIM] = (att * sz_s[:, ls:ls + 2 * HEAD_DIM]).astype(BF16)

    y = x + mod_ref[2:3, :] * _dot(att_s[...], wout_ref[...])
    if final:
        y = _rms(y) * fg_ref[...]
    o_ref[...] = y


def _na_ctx_layer(x, mod, g, w_in, w_out, layer_j, n_layers, seq_len, final_g=None, caches=None):
    n_tok, d = x.shape[1], D_MODEL
    n_seq = n_tok // seq_len
    const = lambda b: (0, 0)
    final = final_g is not None
    in_specs = [_colblock_spec(seq_len),
                pl.BlockSpec((None, 3, d), lambda b: (0, 0, 0)),
                pl.BlockSpec((1, d), const),
                pl.BlockSpec((d, 4 * d), const),
                pl.BlockSpec((d, d), const)]
    args = [x, mod, g, w_in, w_out]
    aliases = {}
    if final:
        in_specs.append(pl.BlockSpec((1, d), const))
        args.append(final_g)
    if caches is not None:
        aliases = {len(args): 1, len(args) + 1: 2}
        in_specs += [pl.BlockSpec(memory_space=pl.ANY)] * 2
        args += list(caches)
    cache_shape = jax.ShapeDtypeStruct((n_seq, n_layers, seq_len, d), F32)
    cache_spec = pl.BlockSpec((None, None, seq_len, d), lambda b: (b, layer_j, 0, 0))
    return pl.pallas_call(
        functools.partial(_na_ctx_body, final=final, n_extra=len(args) - 5),
        grid=(n_seq,),
        in_specs=in_specs,
        out_specs=[pl.BlockSpec((seq_len, d), lambda b: (b, 0)), cache_spec, cache_spec],
        out_shape=[jax.ShapeDtypeStruct((n_tok, d), F32), cache_shape, cache_shape],
        scratch_shapes=[pltpu.VMEM((seq_len, d), BF16),
                        pltpu.VMEM((seq_len, d), BF16),
                        pltpu.VMEM((seq_len, d), BF16),
                        pltpu.VMEM((seq_len, d), F32),
                        pltpu.VMEM((seq_len, d), BF16)],
        input_output_aliases=aliases,
        compiler_params=_params(("arbitrary",)),
        name="na_ctx_layer",
    )(*args)


def _na_proj_body(x_ref, mod_ref, g_ref, win_ref, q_ref, k_ref, v_ref, sz_ref):
    d = D_MODEL
    gs = g_ref[...] * (1.0 + mod_ref[1:2, :])
    h = _modulate(_load_colblocks(x_ref), gs, mod_ref[0:1, :])
    for cb in range(d // COL_CHUNK):
        cs = cb * COL_CHUNK
        q_ref[:, cs:cs + COL_CHUNK] = (_dot(h, win_ref[:, cs:cs + COL_CHUNK]) * (HEAD_DIM ** -0.5)).astype(BF16)
        k_ref[:, cs:cs + COL_CHUNK] = _dot(h, win_ref[:, d + cs:d + cs + COL_CHUNK]).astype(BF16)
        v_ref[:, cs:cs + COL_CHUNK] = _dot(h, win_ref[:, 2 * d + cs:2 * d + cs + COL_CHUNK]).astype(BF16)
        sz_ref[:, cs:cs + COL_CHUNK] = _silu(_dot(h, win_ref[:, 3 * d + cs:3 * d + cs + COL_CHUNK])).astype(BF16)


def _na_proj(x, mod, row0, seq_len, tile, g, w_in):
    n_tok, d = x.shape[1], D_MODEL
    tiles_per_seq = seq_len // tile
    const = lambda i: (0, 0)
    tok_spec = pl.BlockSpec((tile, d), lambda i: (i, 0))
    out = jax.ShapeDtypeStruct((n_tok, d), BF16)
    return pl.pallas_call(
        _na_proj_body,
        grid=(n_tok // tile,),
        in_specs=[_colblock_spec(tile),
                  _mod_spec(row0, 1, tiles_per_seq),
                  pl.BlockSpec((1, d), const),
                  pl.BlockSpec((d, 4 * d), const)],
        out_specs=[tok_spec] * 4,
        out_shape=[out] * 4,
        compiler_params=_params(("arbitrary",)),
        name="na_proj",
    )(x, mod, g, w_in)


def _block_rows(qb):
    ks = min(max(Q_ROWS * qb - ROW_WIN // 2, 0), GRID_ROWS - K_ROWS)
    rs = [min(max(Q_ROWS * qb + qr - ROW_WIN // 2, 0), GRID_ROWS - ROW_WIN) for qr in range(Q_ROWS)]
    return ks, rs


_CLASS_BLOCKS = (0, 2, N_QBLK - 1)


def _na_att_body(q_ref, k_ref, v_ref, kc_ref, vc_ref, t2_ref, o_ref, bias_s, kc_s, vc_s):
    masks = _head_masks()
    first = masks[0]
    kc_s[...] = kc_ref[...].astype(BF16)
    vc_s[...] = vc_ref[...].astype(BF16)

    neg = jnp.full((GRID_W, 2 * GRID_W), NEG_INF, F32)
    for cls, qb in enumerate(_CLASS_BLOCKS):
        ks, rs = _block_rows(qb)
        for hh in range(2):
            for qr in range(Q_ROWS):
                r = Q_ROWS * qb + qr
                for m in range(K_ROWS // 2):
                    ka = ks + 2 * m
                    va = rs[qr] <= ka < rs[qr] + ROW_WIN
                    vb = rs[qr] <= ka + 1 < rs[qr] + ROW_WIN
                    if va or vb:
                        t = t2_ref[hh, ka - r + (ROW_WIN - 1) + 1]
                        if not va:
                            t = jnp.where(first, neg, t)
                        if not vb:
                            t = jnp.where(first, t, neg)
                    else:
                        t = neg
                    bias_s[cls * 2 + hh, qr * GRID_W:(qr + 1) * GRID_W, m * 2 * GRID_W:(m + 1) * 2 * GRID_W] = t

    def block(q0, k0, cls):
        q = q_ref[pl.ds(q0, Q_BLK), :]
        kl = k_ref[pl.ds(k0, K_BLK), :]
        vl = v_ref[pl.ds(k0, K_BLK), :]
        outs = []
        for hh in range(2):
            qm = jnp.where(masks[hh], q, jnp.zeros_like(q))
            sl = _dot_nt(qm, kl) + bias_s[cls * 2 + hh]
            sc = _dot_nt(qm, kc_s[...])
            mx = jnp.maximum(jnp.max(sl, axis=-1, keepdims=True), jnp.max(sc, axis=-1, keepdims=True))
            p_l = jnp.exp(sl - mx)
            p_c = jnp.exp(sc - mx)
            l = jnp.sum(p_l, axis=-1, keepdims=True) + jnp.sum(p_c, axis=-1, keepdims=True)
            outs.append((_dot(p_l.astype(BF16), vl) + _dot(p_c.astype(BF16), vc_s[...])) / l)
        o_ref[pl.ds(q0, Q_BLK), :] = jnp.where(first, outs[0], outs[1]).astype(BF16)

    block(0, _block_rows(0)[0] * GRID_W, 0)

    def interior(qb, carry):
        q0 = pl.multiple_of(qb * Q_BLK, Q_BLK)
        block(q0, pl.multiple_of(q0 - (ROW_WIN // 2) * GRID_W, Q_BLK), 1)
        return carry

    lax.fori_loop(1, N_QBLK - 1, interior, 0)
    block((N_QBLK - 1) * Q_BLK, _block_rows(N_QBLK - 1)[0] * GRID_W, 2)


def _na_att(q, k, v, cache_k, cache_v, t2, layer_j, n_batch):
    n_tok, d = q.shape
    seq_len = n_tok // n_batch
    past = cache_k.shape[2]
    pw = 2 * HEAD_DIM
    tok_spec = pl.BlockSpec((seq_len, pw), lambda hp, b: (b, hp))
    cache_spec = pl.BlockSpec((None, None, past, pw), lambda hp, b: (b, layer_j, 0, hp))
    return pl.pallas_call(
        _na_att_body,
        grid=(N_HEADS // 2, n_batch),
        in_specs=[tok_spec, tok_spec, tok_spec, cache_spec, cache_spec,
                  pl.BlockSpec((None, 2, 2 * ROW_WIN, GRID_W, pw), lambda hp, b: (layer_j, hp, 0, 0, 0))],
        out_specs=tok_spec,
        out_shape=jax.ShapeDtypeStruct((n_tok, d), BF16),
        scratch_shapes=[pltpu.VMEM((3 * 2, Q_BLK, K_BLK), F32),
                        pltpu.VMEM((past, pw), BF16),
                        pltpu.VMEM((past, pw), BF16)],
        compiler_params=_params(("arbitrary", "arbitrary")),
        name="na_attention",
    )(q, k, v, cache_k, cache_v, t2)


def _na_out_body(*refs, final):
    if final:
        x_ref, att_ref, sz_ref, mod_ref, wout_ref, fg_ref, o_ref = refs
    else:
        x_ref, att_ref, sz_ref, mod_ref, wout_ref, o_ref = refs
    a = (att_ref[...].astype(F32) * sz_ref[...].astype(F32)).astype(BF16)
    y = _load_colblocks(x_ref) + mod_ref[2:3, :] * _dot(a, wout_ref[...])
    if final:
        y = _rms(y) * fg_ref[...]
    o_ref[...] = y


def _na_out(x, att, sz, mod, row0, seq_len, tile, w_out, final_g=None):
    n_tok, d = x.shape[1], D_MODEL
    tiles_per_seq = seq_len // tile
    const = lambda i: (0, 0)
    tok_spec = pl.BlockSpec((tile, d), lambda i: (i, 0))
    final = final_g is not None
    in_specs = [_colblock_spec(tile), tok_spec, tok_spec,
                _mod_spec(row0, 1, tiles_per_seq),
                pl.BlockSpec((d, d), const)]
    args = [x, att, sz, mod, w_out]
    if final:
        in_specs.append(pl.BlockSpec((1, d), const))
        args.append(final_g)
    return pl.pallas_call(
        functools.partial(_na_out_body, final=final),
        grid=(n_tok // tile,),
        in_specs=in_specs,
        out_specs=tok_spec,
        out_shape=jax.ShapeDtypeStruct((n_tok, d), F32),
        compiler_params=_params(("arbitrary",)),
        name="na_out",
    )(*args)


def _bias_tables(rpb):
    qc = np.arange(GRID_W)[:, None]
    kc = np.arange(GRID_W)[None, :]
    start = np.clip(qc - COL_WIN // 2, 0, GRID_W - COL_WIN)
    in_win = (kc >= start) & (kc < start + COL_WIN)
    dc = kc - qc + COL_WIN - 1
    onehot = (np.arange(2 * COL_WIN - 1)[:, None, None] == dc[None]) & in_win[None]
    c = jnp.einsum("lhdj,jqk->lhdqk", rpb, jnp.asarray(onehot, F32), precision=lax.Precision.HIGHEST)
    c = jnp.where(jnp.asarray(in_win), c, NEG_INF)
    neg = jnp.full(c.shape[:2] + (1, GRID_W, GRID_W), NEG_INF, F32)
    cx = jnp.concatenate([neg, c, neg], axis=2)
    return jnp.concatenate([cx[:, :, :2 * ROW_WIN], cx[:, :, 1:2 * ROW_WIN + 1]], axis=-1)


CONV_TILE = 1024
NA_TILE = 512


def kernel(x_prompt, x_sample, c, cache_k, cache_v, c_ctx, norm_g, ada_w, ada_b, conv_w_in, conv_dw_w,
           conv_dw_b, conv_ln_g, conv_ln_b, conv_w_out, na_w_in, na_rpb, na_w_out, final_g):
    n_ctx, ctx_len, d = x_prompt.shape
    n_dec, dec_len, _ = x_sample.shape
    n_na = na_w_in.shape[0]
    past = cache_k.shape[2]

    cvec = jnp.concatenate([c_ctx[None], c, jnp.zeros((MOD_ROWS - 1 - n_dec, d), F32)], axis=0)
    mod = _ada(cvec, ada_w, ada_b).reshape(DEPTH, MOD_ROWS, 3, d)

    xc = x_prompt.reshape(n_ctx * ctx_len, d)
    xs = x_sample.reshape(n_dec * dec_len, d)
    ck = cache_k.reshape(n_dec, n_na, past, d)
    cv = cache_v.reshape(n_dec, n_na, past, d)
    fg = final_g.reshape(1, d)
    bias_tables = _bias_tables(na_rpb)
    caches = None
    for i in range(DEPTH):
        j = i // 2
        g = norm_g[i].reshape(1, d)
        if i % 2 == 0:
            wi = conv_w_in[j].astype(BF16)
            wo = conv_w_out[j].astype(BF16)
            rest = (g, wi, conv_dw_w[j], conv_dw_b[j].reshape(1, d), conv_ln_g[j].reshape(1, d),
                    conv_ln_b[j].reshape(1, d), wo)
            xc = _conv_layer(xc, mod[i], 0, 0, ctx_len, CONV_TILE, *rest)
            xs = _conv_layer(xs, mod[i], 1, 1, dec_len, CONV_TILE, *rest)
        else:
            wi = na_w_in[j].astype(BF16)
            wo = na_w_out[j].astype(BF16)
            last = i == DEPTH - 1
            xc, new_k, new_v = _na_ctx_layer(xc, mod[i], g, wi, wo, j, n_na, ctx_len,
                                             final_g=fg if last else None, caches=caches)
            caches = (new_k, new_v)
            q, k, v, sz = _na_proj(xs, mod[i], 1, dec_len, NA_TILE, g, wi)
            att = _na_att(q, k, v, ck, cv, bias_tables, j, n_dec)
            xs = _na_out(xs, att, sz, mod[i], 1, dec_len, NA_TILE, wo, final_g=fg if last else None)

    y_prompt = xc.reshape(n_ctx, ctx_len, d)
    y_sample = xs.reshape(n_dec, dec_len, d)
    new_cache_k = caches[0].reshape(n_ctx, n_na, ctx_len, N_HEADS, HEAD_DIM)
    new_cache_v = caches[1].reshape(n_ctx, n_na, ctx_len, N_HEADS, HEAD_DIM)
    return (y_prompt, y_sample, new_cache_k, new_cache_v)
```

```python
import functools

import numpy as np
import jax
import jax.numpy as jnp
from jax import lax
from jax.experimental import pallas as pl
from jax.experimental.pallas import tpu as pltpu

F32 = jnp.float32
BF16 = jnp.bfloat16

D_MODEL = 1024
LANES = 128
N_COLB = D_MODEL // LANES
DEPTH = 4
N_HEADS = 16
HEAD_DIM = 64
GRID_W = 64
GRID_ROWS = 32
CONV_K = 31
CONV_HALO = 16
ROW_WIN = 8
COL_WIN = 16
EPS = 1e-6
NEG_INF = -1e30
MOD_ROWS = 16

Q_ROWS = 4
Q_BLK = Q_ROWS * GRID_W
K_ROWS = 12
K_BLK = K_ROWS * GRID_W
N_QBLK = GRID_ROWS // Q_ROWS

VMEM_LIMIT_BYTES = 56 * 1024 * 1024


def _params(sem):
    return pltpu.CompilerParams(dimension_semantics=sem, vmem_limit_bytes=VMEM_LIMIT_BYTES)


def _sigmoid(x):
    return 1.0 / (1.0 + jnp.exp(-x))


def _silu(x):
    return x * _sigmoid(x)


def _rms(x):
    return x * lax.rsqrt(jnp.mean(x * x, axis=-1, keepdims=True) + EPS)


def _dot(a, b):
    return jnp.dot(a, b, preferred_element_type=F32)


def _dot_nt(a, b):
    return lax.dot_general(a, b, (((1,), (1,)), ((), ())), preferred_element_type=F32)


ADA_TILE = 768


def _ada_body(c_ref, w_ref, b_ref, o_ref):
    s = _silu(c_ref[...]).astype(BF16)
    o_ref[...] = _dot(s, w_ref[...].astype(BF16)) + b_ref[...]


def _ada(cvec, ada_w, ada_b):
    n_out = 3 * D_MODEL
    return pl.pallas_call(
        _ada_body,
        grid=(DEPTH, n_out // ADA_TILE),
        in_specs=[pl.BlockSpec((MOD_ROWS, D_MODEL), lambda l, n: (0, 0)),
                  pl.BlockSpec((None, D_MODEL, ADA_TILE), lambda l, n: (l, 0, n)),
                  pl.BlockSpec((None, 1, ADA_TILE), lambda l, n: (l, 0, n))],
        out_specs=pl.BlockSpec((None, MOD_ROWS, ADA_TILE), lambda l, n: (l, 0, n)),
        out_shape=jax.ShapeDtypeStruct((DEPTH, MOD_ROWS, n_out), F32),
        compiler_params=_params(("arbitrary", "arbitrary")),
        name="ada_params",
    )(cvec, ada_w, ada_b.reshape(DEPTH, 1, n_out))


def _modulate(x, gs, shift):
    return (_rms(x) * gs + shift).astype(BF16)


def _colblock_spec(rows):
    return pl.BlockSpec((N_COLB, rows, LANES), lambda i: (0, i, 0))


def _load_colblocks(x_ref):
    return jnp.concatenate([x_ref[cb] for cb in range(N_COLB)], axis=-1)


N_SEG = 8
PRE_ROWS = 16
ROW_CHUNK = 128
COL_W = 256
NORM_ROWS = 32
COL_CHUNK = 512


def _conv_body(*refs, tile, ext_halo, tiles_per_seq, segs_per_seq):
    x_refs, refs = refs[:N_COLB], refs[N_COLB:]
    if ext_halo:
        (xp_ref, xn_ref, mod_ref, g_ref, win_ref, dww_ref, dwb_ref, lng_ref, lnb_ref, wout_ref,
         o_ref, h_s, u_s, sz_s, gt_s, v_s, wb_s) = refs
    else:
        (mod_ref, g_ref, win_ref, dww_ref, dwb_ref, lng_ref, lnb_ref, wout_ref,
         o_ref, h_s, u_s, sz_s, gt_s, v_s, wb_s) = refs
    d = D_MODEL
    seg = tile // N_SEG
    halo_rows = CONV_HALO * N_SEG
    main0 = halo_rows
    back0 = main0 + tile
    shift = mod_ref[0:1, :]
    gate = mod_ref[2:3, :]
    gs = g_ref[...] * (1.0 + mod_ref[1:2, :])

    def seg_rows(a):
        return pl.ds(a, N_SEG, stride=seg)

    def load_x(a):
        return jnp.concatenate([xr[seg_rows(a), :] for xr in x_refs], axis=-1)

    def pre(c, carry):
        a = c * (PRE_ROWS // N_SEG)
        xs = jnp.concatenate([load_x(a + j) for j in range(PRE_ROWS // N_SEG)], axis=0)
        h_s[pl.ds(pl.multiple_of(c * PRE_ROWS, PRE_ROWS), PRE_ROWS), :] = _modulate(xs, gs, shift)
        return carry

    lax.fori_loop(0, tile // PRE_ROWS, pre, 0, unroll=8)
    n_rc = tile // ROW_CHUNK
    n_cc = d // COL_W
    for k in range(CONV_K):
        wb_s[k * N_SEG:(k + 1) * N_SEG, :] = jnp.broadcast_to(dww_ref[k:k + 1, :], (N_SEG, d))
    wb_s[CONV_K * N_SEG:(CONV_K + 1) * N_SEG, :] = jnp.broadcast_to(dwb_ref[...], (N_SEG, d))

    def row0(r):
        return r * ROW_CHUNK if isinstance(r, int) else pl.multiple_of(r * ROW_CHUNK, ROW_CHUNK)

    def chunk_rows(r):
        return pl.ds(row0(r), ROW_CHUNK)

    sub = lax.broadcasted_iota(jnp.int32, (N_SEG, 1), 0)
    if ext_halo:
        i = pl.program_id(0)
        h_s[tile:tile + CONV_HALO, :] = _modulate(xp_ref[...], gs, shift)
        h_s[tile + CONV_HALO:tile + 2 * CONV_HALO, :] = _modulate(xn_ref[...], gs, shift)
        he = h_s[tile:tile + 2 * CONV_HALO, :]
        for cb in range(d // COL_CHUNK):
            cs = cb * COL_CHUNK
            a = _dot(he, win_ref[:, cs:cs + COL_CHUNK])
            b = _dot(he, win_ref[:, d + cs:d + cs + COL_CHUNK])
            u_s[0:2 * CONV_HALO, cs:cs + COL_CHUNK] = a * _sigmoid(b)
        starts_seq = sub == 0
        ends_seq = sub == N_SEG - 1
    else:
        starts_seq = sub % segs_per_seq == 0
        ends_seq = sub % segs_per_seq == segs_per_seq - 1

    def fill_halo(j):
        cols = slice(j * COL_W, (j + 1) * COL_W)
        if ext_halo:
            u_prev = jnp.where(i % tiles_per_seq == 0, 0.0, u_s[0:CONV_HALO, cols])
            u_next = jnp.where(i % tiles_per_seq == tiles_per_seq - 1, 0.0, u_s[CONV_HALO:2 * CONV_HALO, cols])
        front = u_s[back0 - halo_rows - 1:back0 - 1, cols]
        back = u_s[main0 + 1:main0 + 1 + halo_rows, cols]
        for p in range(CONV_HALO):
            rows = slice(p * N_SEG, (p + 1) * N_SEG)
            edge_f = jnp.broadcast_to(u_prev[p:p + 1, :], (N_SEG, COL_W)) if ext_halo else 0.0
            edge_b = jnp.broadcast_to(u_next[p:p + 1, :], (N_SEG, COL_W)) if ext_halo else 0.0
            u_s[p * N_SEG:(p + 1) * N_SEG, cols] = jnp.where(starts_seq, edge_f, front[rows])
            u_s[back0 + p * N_SEG:back0 + (p + 1) * N_SEG, cols] = jnp.where(ends_seq, edge_b, back[rows])

    def conv_chunk(j, r):
        r0 = row0(r)
        for cb in range(j * COL_W // LANES, (j + 1) * COL_W // LANES):
            lanes = slice(cb * LANES, (cb + 1) * LANES)
            n_pos = ROW_CHUNK // N_SEG
            accs = [wb_s[CONV_K * N_SEG:(CONV_K + 1) * N_SEG, lanes]] * n_pos
            for k in range(CONV_K):
                wk = wb_s[k * N_SEG:(k + 1) * N_SEG, lanes]
                k0 = r0 + (k + CONV_HALO - CONV_K // 2) * N_SEG
                accs = [acc + u_s[pl.ds(k0 + p * N_SEG, N_SEG), lanes] * wk for p, acc in enumerate(accs)]
            v_s[pl.ds(r0, ROW_CHUNK), lanes] = jnp.concatenate(accs, axis=0)

    for cb in range(d // COL_CHUNK):
        cs = cb * COL_CHUNK
        hm = h_s[0:tile, :]
        a = _dot(hm, win_ref[:, cs:cs + COL_CHUNK])
        b = _dot(hm, win_ref[:, d + cs:d + cs + COL_CHUNK])
        u_s[main0:main0 + tile, cs:cs + COL_CHUNK] = a * _sigmoid(b)
        z = _dot(hm, win_ref[:, 2 * d + cs:2 * d + cs + COL_CHUNK])
        sz_s[:, cs:cs + COL_CHUNK] = _silu(z)
    for j in range(n_cc):
        fill_halo(j)

    def conv_rows(r, carry):
        for j in range(n_cc):
            conv_chunk(j, r)
        return carry

    def norm_gate(r):
        for part in range(ROW_CHUNK // NORM_ROWS):
            rows = pl.ds(row0(r) + part * NORM_ROWS, NORM_ROWS)
            acc = v_s[rows, :]
            mu = jnp.mean(acc, axis=-1, keepdims=True)
            dev = acc - mu
            var = jnp.mean(dev * dev, axis=-1, keepdims=True)
            y = dev * lax.rsqrt(var + EPS) * lng_ref[...] + lnb_ref[...]
            gt_s[rows, :] = (_silu(y) * sz_s[rows, :]).astype(BF16)

    def out_proj(r):
        rows = chunk_rows(r)
        u_s[rows, :] = gate * _dot(gt_s[rows, :], wout_ref[...])

    def norm_and_project(r, carry):
        norm_gate(r)
        out_proj(r - 1)
        return carry

    lax.fori_loop(0, n_rc, conv_rows, 0)
    norm_gate(0)
    lax.fori_loop(1, n_rc, norm_and_project, 0)
    out_proj(n_rc - 1)

    def post(a, carry):
        rows = seg_rows(a)
        y = u_s[pl.ds(pl.multiple_of(a * N_SEG, N_SEG), N_SEG), :]
        for cb, xr in enumerate(x_refs):
            o_ref[cb, rows, :] = xr[rows, :] + y[:, cb * LANES:(cb + 1) * LANES]
        return carry

    lax.fori_loop(0, seg, post, 0, unroll=16)


def _mod_spec(row0, row_step, tiles_per_seq):
    return pl.BlockSpec((None, 3, D_MODEL), lambda i: (row0 + row_step * (i // tiles_per_seq), 0, 0))


def _resident(shape):
    return pl.BlockSpec(shape, lambda *_: (0,) * len(shape), pipeline_mode=pl.Buffered(1))


def _conv_layer(x, mod, row0, row_step, seq_len, tile, g, w_in, dw_w, dw_b, ln_g, ln_b, w_out):
    n_tok, d = x.shape
    seg = tile // N_SEG
    segs_per_seq = seq_len // seg
    ext_halo = segs_per_seq > N_SEG
    tiles_per_seq = max(segs_per_seq // N_SEG, 1)
    assert (segs_per_seq % N_SEG == 0) if ext_halo else (N_SEG % segs_per_seq == 0)
    hb = tile // CONV_HALO
    n_hblk = n_tok // CONV_HALO
    in_specs = [pl.BlockSpec((tile, LANES), lambda i, cb=cb: (i, cb)) for cb in range(N_COLB)]
    args = [x] * N_COLB
    if ext_halo:
        in_specs += [pl.BlockSpec((CONV_HALO, d), lambda i: (jnp.maximum(i * hb - 1, 0), 0)),
                     pl.BlockSpec((CONV_HALO, d), lambda i: (jnp.minimum((i + 1) * hb, n_hblk - 1), 0))]
        args += [x, x]
    in_specs += [_mod_spec(row0, row_step, tiles_per_seq),
                 _resident((1, d)), _resident((d, 3 * d)), _resident((CONV_K, d)),
                 _resident((1, d)), _resident((1, d)), _resident((1, d)), _resident((d, d))]
    args += [mod, g, w_in, dw_w, dw_b, ln_g, ln_b, w_out]
    h_rows = tile + 2 * CONV_HALO if ext_halo else tile
    return pl.pallas_call(
        functools.partial(_conv_body, tile=tile, ext_halo=ext_halo, tiles_per_seq=tiles_per_seq,
                          segs_per_seq=segs_per_seq),
        grid=(n_tok // tile,),
        in_specs=in_specs,
        out_specs=pl.BlockSpec((N_COLB, tile, LANES), lambda i: (0, i, 0)),
        out_shape=jax.ShapeDtypeStruct((N_COLB, n_tok, LANES), F32),
        scratch_shapes=[pltpu.VMEM((h_rows, d), BF16),
                        pltpu.VMEM((tile + 2 * CONV_HALO * N_SEG, d), F32),
                        pltpu.VMEM((tile, d), F32),
                        pltpu.VMEM((tile, d), BF16),
                        pltpu.VMEM((tile, d), F32),
                        pltpu.VMEM(((CONV_K + 1) * N_SEG, d), F32)],
        compiler_params=_params(("arbitrary",)),
        name="conv_layer",
    )(*args)


def _head_masks():
    lane = lax.broadcasted_iota(jnp.int32, (1, 2 * HEAD_DIM), 1)
    first = lane < HEAD_DIM
    return first, jnp.logical_not(first)


def _na_ctx_body(*refs, final, n_extra):
    x_ref, mod_ref, g_ref, win_ref, wout_ref = refs[:5]
    fg_ref = refs[5] if final else None
    o_ref, k_ref, v_ref, q_s, kb_s, vb_s, sz_s, att_s = refs[5 + n_extra:]
    d = D_MODEL
    x = _load_colblocks(x_ref)
    gs = g_ref[...] * (1.0 + mod_ref[1:2, :])
    h = _modulate(x, gs, mod_ref[0:1, :])
    for cb in range(d // COL_CHUNK):
        cs = cb * COL_CHUNK
        q_s[:, cs:cs + COL_CHUNK] = (_dot(h, win_ref[:, cs:cs + COL_CHUNK]) * (HEAD_DIM ** -0.5)).astype(BF16)
        k = _dot(h, win_ref[:, d + cs:d + cs + COL_CHUNK])
        k_ref[:, cs:cs + COL_CHUNK] = k
        kb_s[:, cs:cs + COL_CHUNK] = k.astype(BF16)
        v = _dot(h, win_ref[:, 2 * d + cs:2 * d + cs + COL_CHUNK])
        v_ref[:, cs:cs + COL_CHUNK] = v
        vb_s[:, cs:cs + COL_CHUNK] = v.astype(BF16)
        sz_s[:, cs:cs + COL_CHUNK] = _silu(_dot(h, win_ref[:, 3 * d + cs:3 * d + cs + COL_CHUNK]))

    masks = _head_masks()
    for hp in range(N_HEADS // 2):
        ls = hp * 2 * HEAD_DIM
        q = q_s[:, ls:ls + 2 * HEAD_DIM]
        kp = kb_s[:, ls:ls + 2 * HEAD_DIM]
        vp = vb_s[:, ls:ls + 2 * HEAD_DIM]
        outs = []
        for hh in range(2):
            s = _dot_nt(jnp.where(masks[hh], q, jnp.zeros_like(q)), kp)
            p = jnp.exp(s - jnp.max(s, axis=-1, keepdims=True))
            l = jnp.sum(p, axis=-1, keepdims=True)
            outs.append(_dot(p.astype(BF16), vp) / l)
        att = jnp.where(masks[0], outs[0], outs[1])
        att_s[:, ls:ls + 2 * HEAD_DIM] = (att * sz_s[:, ls:ls + 2 * HEAD_DIM]).astype(BF16)

    y = x + mod_ref[2:3, :] * _dot(att_s[...], wout_ref[...])
    if final:
        y = _rms(y) * fg_ref[...]
    o_ref[...] = y


def _na_ctx_layer(x, mod, g, w_in, w_out, layer_j, n_layers, seq_len, final_g=None, caches=None):
    n_tok, d = x.shape[1], D_MODEL
    n_seq = n_tok // seq_len
    const = lambda b: (0, 0)
    final = final_g is not None
    in_specs = [_colblock_spec(seq_len),
                pl.BlockSpec((None, 3, d), lambda b: (0, 0, 0)),
                pl.BlockSpec((1, d), const),
                pl.BlockSpec((d, 4 * d), const),
                pl.BlockSpec((d, d), const)]
    args = [x, mod, g, w_in, w_out]
    aliases = {}
    if final:
        in_specs.append(pl.BlockSpec((1, d), const))
        args.append(final_g)
    if caches is not None:
        aliases = {len(args): 1, len(args) + 1: 2}
        in_specs += [pl.BlockSpec(memory_space=pl.ANY)] * 2
        args += list(caches)
    cache_shape = jax.ShapeDtypeStruct((n_seq, n_layers, seq_len, d), F32)
    cache_spec = pl.BlockSpec((None, None, seq_len, d), lambda b: (b, layer_j, 0, 0))
    return pl.pallas_call(
        functools.partial(_na_ctx_body, final=final, n_extra=len(args) - 5),
        grid=(n_seq,),
        in_specs=in_specs,
        out_specs=[pl.BlockSpec((seq_len, d), lambda b: (b, 0)), cache_spec, cache_spec],
        out_shape=[jax.ShapeDtypeStruct((n_tok, d), F32), cache_shape, cache_shape],
        scratch_shapes=[pltpu.VMEM((seq_len, d), BF16),
                        pltpu.VMEM((seq_len, d), BF16),
                        pltpu.VMEM((seq_len, d), BF16),
                        pltpu.VMEM((seq_len, d), F32),
                        pltpu.VMEM((seq_len, d), BF16)],
        input_output_aliases=aliases,
        compiler_params=_params(("arbitrary",)),
        name="na_ctx_layer",
    )(*args)


def _na_proj_body(x_ref, mod_ref, g_ref, win_ref, q_ref, k_ref, v_ref, sz_ref):
    d = D_MODEL
    gs = g_ref[...] * (1.0 + mod_ref[1:2, :])
    h = _modulate(_load_colblocks(x_ref), gs, mod_ref[0:1, :])
    for cb in range(d // COL_CHUNK):
        cs = cb * COL_CHUNK
        q_ref[:, cs:cs + COL_CHUNK] = (_dot(h, win_ref[:, cs:cs + COL_CHUNK]) * (HEAD_DIM ** -0.5)).astype(BF16)
        k_ref[:, cs:cs + COL_CHUNK] = _dot(h, win_ref[:, d + cs:d + cs + COL_CHUNK]).astype(BF16)
        v_ref[:, cs:cs + COL_CHUNK] = _dot(h, win_ref[:, 2 * d + cs:2 * d + cs + COL_CHUNK]).astype(BF16)
        sz_ref[:, cs:cs + COL_CHUNK] = _silu(_dot(h, win_ref[:, 3 * d + cs:3 * d + cs + COL_CHUNK])).astype(BF16)


def _na_proj(x, mod, row0, seq_len, tile, g, w_in):
    n_tok, d = x.shape[1], D_MODEL
    tiles_per_seq = seq_len // tile
    const = lambda i: (0, 0)
    tok_spec = pl.BlockSpec((tile, d), lambda i: (i, 0))
    out = jax.ShapeDtypeStruct((n_tok, d), BF16)
    return pl.pallas_call(
        _na_proj_body,
        grid=(n_tok // tile,),
        in_specs=[_colblock_spec(tile),
                  _mod_spec(row0, 1, tiles_per_seq),
                  pl.BlockSpec((1, d), const),
                  pl.BlockSpec((d, 4 * d), const)],
        out_specs=[tok_spec] * 4,
        out_shape=[out] * 4,
        compiler_params=_params(("arbitrary",)),
        name="na_proj",
    )(x, mod, g, w_in)


def _block_rows(qb):
    ks = min(max(Q_ROWS * qb - ROW_WIN // 2, 0), GRID_ROWS - K_ROWS)
    rs = [min(max(Q_ROWS * qb + qr - ROW_WIN // 2, 0), GRID_ROWS - ROW_WIN) for qr in range(Q_ROWS)]
    return ks, rs


_CLASS_BLOCKS = (0, 2, N_QBLK - 1)


def _na_att_body(q_ref, k_ref, v_ref, kc_ref, vc_ref, t2_ref, o_ref, bias_s, kc_s, vc_s):
    masks = _head_masks()
    first = masks[0]
    kc_s[...] = kc_ref[...].astype(BF16)

    def with_ones(v, hh):
        return jnp.where(masks[hh], v, jnp.ones_like(v))

    vc = vc_ref[...].astype(BF16)
    for hh in range(2):
        vc_s[hh] = with_ones(vc, hh)

    neg = jnp.full((GRID_W, 2 * GRID_W), NEG_INF, F32)
    for cls, qb in enumerate(_CLASS_BLOCKS):
        ks, rs = _block_rows(qb)
        for hh in range(2):
            for qr in range(Q_ROWS):
                r = Q_ROWS * qb + qr
                for m in range(K_ROWS // 2):
                    ka = ks + 2 * m
                    va = rs[qr] <= ka < rs[qr] + ROW_WIN
                    vb = rs[qr] <= ka + 1 < rs[qr] + ROW_WIN
                    if va or vb:
                        t = t2_ref[hh, ka - r + (ROW_WIN - 1) + 1]
                        if not va:
                            t = jnp.where(first, neg, t)
                        if not vb:
                            t = jnp.where(first, t, neg)
                    else:
                        t = neg
                    bias_s[cls * 2 + hh, qr * GRID_W:(qr + 1) * GRID_W, m * 2 * GRID_W:(m + 1) * 2 * GRID_W] = t

    def block(q0, k0, cls):
        q = q_ref[pl.ds(q0, Q_BLK), :]
        kl = k_ref[pl.ds(k0, K_BLK), :]
        vl = v_ref[pl.ds(k0, K_BLK), :]
        outs = []
        for hh in range(2):
            qm = jnp.where(masks[hh], q, jnp.zeros_like(q))
            sl = _dot_nt(qm, kl) + bias_s[cls * 2 + hh]
            sc = _dot_nt(qm, kc_s[...])
            mx = jnp.maximum(jnp.max(sl, axis=-1, keepdims=True), jnp.max(sc, axis=-1, keepdims=True))
            p_l = jnp.exp(sl - mx).astype(BF16)
            p_c = jnp.exp(sc - mx).astype(BF16)
            o = _dot(p_l, with_ones(vl, hh)) + _dot(p_c, vc_s[hh])
            outs.append(o / pltpu.roll(o, HEAD_DIM, 1))
        o_ref[pl.ds(q0, Q_BLK), :] = jnp.where(first, outs[0], outs[1]).astype(BF16)

    block(0, _block_rows(0)[0] * GRID_W, 0)

    def interior(qb, carry):
        q0 = pl.multiple_of(qb * Q_BLK, Q_BLK)
        block(q0, pl.multiple_of(q0 - (ROW_WIN // 2) * GRID_W, Q_BLK), 1)
        return carry

    lax.fori_loop(1, N_QBLK - 1, interior, 0, unroll=2)
    block((N_QBLK - 1) * Q_BLK, _block_rows(N_QBLK - 1)[0] * GRID_W, 2)


def _na_att(q, k, v, cache_k, cache_v, t2, layer_j, n_batch):
    n_tok, d = q.shape
    seq_len = n_tok // n_batch
    past = cache_k.shape[2]
    pw = 2 * HEAD_DIM
    tok_spec = pl.BlockSpec((seq_len, pw), lambda hp, b: (b, hp))
    cache_spec = pl.BlockSpec((None, None, past, pw), lambda hp, b: (b, layer_j, 0, hp))
    return pl.pallas_call(
        _na_att_body,
        grid=(N_HEADS // 2, n_batch),
        in_specs=[tok_spec, tok_spec, tok_spec, cache_spec, cache_spec,
                  pl.BlockSpec((None, 2, 2 * ROW_WIN, GRID_W, pw), lambda hp, b: (layer_j, hp, 0, 0, 0))],
        out_specs=tok_spec,
        out_shape=jax.ShapeDtypeStruct((n_tok, d), BF16),
        scratch_shapes=[pltpu.VMEM((3 * 2, Q_BLK, K_BLK), F32),
                        pltpu.VMEM((past, pw), BF16),
                        pltpu.VMEM((2, past, pw), BF16)],
        compiler_params=_params(("arbitrary", "arbitrary")),
        name="na_attention",
    )(q, k, v, cache_k, cache_v, t2)


def _na_out_body(*refs, final):
    if final:
        x_ref, att_ref, sz_ref, mod_ref, wout_ref, fg_ref, o_ref = refs
    else:
        x_ref, att_ref, sz_ref, mod_ref, wout_ref, o_ref = refs
    a = (att_ref[...].astype(F32) * sz_ref[...].astype(F32)).astype(BF16)
    y = _load_colblocks(x_ref) + mod_ref[2:3, :] * _dot(a, wout_ref[...])
    if final:
        y = _rms(y) * fg_ref[...]
    o_ref[...] = y


def _na_out(x, att, sz, mod, row0, seq_len, tile, w_out, final_g=None):
    n_tok, d = x.shape[1], D_MODEL
    tiles_per_seq = seq_len // tile
    const = lambda i: (0, 0)
    tok_spec = pl.BlockSpec((tile, d), lambda i: (i, 0))
    final = final_g is not None
    in_specs = [_colblock_spec(tile), tok_spec, tok_spec,
                _mod_spec(row0, 1, tiles_per_seq),
                pl.BlockSpec((d, d), const)]
    args = [x, att, sz, mod, w_out]
    if final:
        in_specs.append(pl.BlockSpec((1, d), const))
        args.append(final_g)
    return pl.pallas_call(
        functools.partial(_na_out_body, final=final),
        grid=(n_tok // tile,),
        in_specs=in_specs,
        out_specs=tok_spec,
        out_shape=jax.ShapeDtypeStruct((n_tok, d), F32),
        compiler_params=_params(("arbitrary",)),
        name="na_out",
    )(*args)


def _bias_tables(rpb):
    qc = np.arange(GRID_W)[:, None]
    kc = np.arange(GRID_W)[None, :]
    start = np.clip(qc - COL_WIN // 2, 0, GRID_W - COL_WIN)
    in_win = (kc >= start) & (kc < start + COL_WIN)
    dc = kc - qc + COL_WIN - 1
    onehot = (np.arange(2 * COL_WIN - 1)[:, None, None] == dc[None]) & in_win[None]
    c = jnp.einsum("lhdj,jqk->lhdqk", rpb, jnp.asarray(onehot, F32), precision=lax.Precision.HIGHEST)
    c = jnp.where(jnp.asarray(in_win), c, NEG_INF)
    neg = jnp.full(c.shape[:2] + (1, GRID_W, GRID_W), NEG_INF, F32)
    cx = jnp.concatenate([neg, c, neg], axis=2)
    return jnp.concatenate([cx[:, :, :2 * ROW_WIN], cx[:, :, 1:2 * ROW_WIN + 1]], axis=-1)


CONV_TILE = 1024
NA_TILE = 512


def kernel(x_prompt, x_sample, c, cache_k, cache_v, c_ctx, norm_g, ada_w, ada_b, conv_w_in, conv_dw_w,
           conv_dw_b, conv_ln_g, conv_ln_b, conv_w_out, na_w_in, na_rpb, na_w_out, final_g):
    n_ctx, ctx_len, d = x_prompt.shape
    n_dec, dec_len, _ = x_sample.shape
    n_na = na_w_in.shape[0]
    past = cache_k.shape[2]

    cvec = jnp.concatenate([c_ctx[None], c, jnp.zeros((MOD_ROWS - 1 - n_dec, d), F32)], axis=0)
    mod = _ada(cvec, ada_w, ada_b).reshape(DEPTH, MOD_ROWS, 3, d)

    xc = x_prompt.reshape(n_ctx * ctx_len, d)
    xs = x_sample.reshape(n_dec * dec_len, d)
    ck = cache_k.reshape(n_dec, n_na, past, d)
    cv = cache_v.reshape(n_dec, n_na, past, d)
    fg = final_g.reshape(1, d)
    bias_tables = _bias_tables(na_rpb)
    caches = None
    for i in range(DEPTH):
        j = i // 2
        g = norm_g[i].reshape(1, d)
        if i % 2 == 0:
            wi = conv_w_in[j].astype(BF16)
            wo = conv_w_out[j].astype(BF16)
            rest = (g, wi, conv_dw_w[j], conv_dw_b[j].reshape(1, d), conv_ln_g[j].reshape(1, d),
                    conv_ln_b[j].reshape(1, d), wo)
            xc = _conv_layer(xc, mod[i], 0, 0, ctx_len, CONV_TILE, *rest)
            xs = _conv_layer(xs, mod[i], 1, 1, dec_len, CONV_TILE, *rest)
        else:
            wi = na_w_in[j].astype(BF16)
            wo = na_w_out[j].astype(BF16)
            last = i == DEPTH - 1
            xc, new_k, new_v = _na_ctx_layer(xc, mod[i], g, wi, wo, j, n_na, ctx_len,
                                             final_g=fg if last else None, caches=caches)
            caches = (new_k, new_v)
            q, k, v, sz = _na_proj(xs, mod[i], 1, dec_len, NA_TILE, g, wi)
            att = _na_att(q, k, v, ck, cv, bias_tables, j, n_dec)
            xs = _na_out(xs, att, sz, mod[i], 1, dec_len, NA_TILE, wo, final_g=fg if last else None)

    y_prompt = xc.reshape(n_ctx, ctx_len, d)
    y_sample = xs.reshape(n_dec, dec_len, d)
    new_cache_k = caches[0].reshape(n_ctx, n_na, ctx_len, N_HEADS, HEAD_DIM)
    new_cache_v = caches[1].reshape(n_ctx, n_na, ctx_len, N_HEADS, HEAD_DIM)
    return (y_prompt, y_sample, new_cache_k, new_cache_v)
```

```python
import functools

import numpy as np
import jax
import jax.numpy as jnp
from jax import lax
from jax.experimental import pallas as pl
from jax.experimental.pallas import tpu as pltpu

F32 = jnp.float32
BF16 = jnp.bfloat16

D_MODEL = 1024
LANES = 128
N_COLB = D_MODEL // LANES
DEPTH = 4
N_HEADS = 16
HEAD_DIM = 64
GRID_W = 64
GRID_ROWS = 32
CONV_K = 31
CONV_HALO = 16
ROW_WIN = 8
COL_WIN = 16
EPS = 1e-6
NEG_INF = -1e30
LOG2_E = 1.4426950408889634
MOD_ROWS = 16

Q_ROWS = 4
Q_BLK = Q_ROWS * GRID_W
K_ROWS = 12
K_BLK = K_ROWS * GRID_W
N_QBLK = GRID_ROWS // Q_ROWS

VMEM_LIMIT_BYTES = 56 * 1024 * 1024


def _params(sem):
    return pltpu.CompilerParams(dimension_semantics=sem, vmem_limit_bytes=VMEM_LIMIT_BYTES)


def _sigmoid(x):
    return 1.0 / (1.0 + jnp.exp(-x))


def _silu(x):
    return x * _sigmoid(x)


def _rms(x):
    return x * lax.rsqrt(jnp.mean(x * x, axis=-1, keepdims=True) + EPS)


def _dot(a, b):
    return jnp.dot(a, b, preferred_element_type=F32)


def _dot_nt(a, b):
    return lax.dot_general(a, b, (((1,), (1,)), ((), ())), preferred_element_type=F32)


ADA_TILE = 768


def _ada_body(c_ref, w_ref, b_ref, o_ref):
    s = _silu(c_ref[...]).astype(BF16)
    o_ref[...] = _dot(s, w_ref[...].astype(BF16)) + b_ref[...]


def _ada(cvec, ada_w, ada_b):
    n_out = 3 * D_MODEL
    return pl.pallas_call(
        _ada_body,
        grid=(DEPTH, n_out // ADA_TILE),
        in_specs=[pl.BlockSpec((MOD_ROWS, D_MODEL), lambda l, n: (0, 0)),
                  pl.BlockSpec((None, D_MODEL, ADA_TILE), lambda l, n: (l, 0, n)),
                  pl.BlockSpec((None, 1, ADA_TILE), lambda l, n: (l, 0, n))],
        out_specs=pl.BlockSpec((None, MOD_ROWS, ADA_TILE), lambda l, n: (l, 0, n)),
        out_shape=jax.ShapeDtypeStruct((DEPTH, MOD_ROWS, n_out), F32),
        compiler_params=_params(("arbitrary", "arbitrary")),
        name="ada_params",
    )(cvec, ada_w, ada_b.reshape(DEPTH, 1, n_out))


def _modulate(x, gs, shift):
    return (_rms(x) * gs + shift).astype(BF16)


N_SEG = 8
CONV_TILE = 1024
SEG_LEN = CONV_TILE // N_SEG
PRE_ROWS = 16
ROW_CHUNK = 128
COL_W = 256
NORM_ROWS = 32
COL_CHUNK = 512


def _segment_specs(n_rows):
    n = n_rows // SEG_LEN
    return [pl.BlockSpec((None, SEG_LEN, D_MODEL),
                         lambda i, s=s: ((i * n + s) // N_SEG, 0, (i * n + s) % N_SEG)) for s in range(n)]


def _load_segments(x_refs):
    return jnp.concatenate([r[...] for r in x_refs], axis=0)


def _conv_body(*refs, tile, ext_halo, tiles_per_seq, segs_per_seq):
    x_refs, refs = refs[:N_COLB], refs[N_COLB:]
    if ext_halo:
        (xp_ref, xn_ref, mod_ref, g_ref, win_ref, dww_ref, dwb_ref, lng_ref, lnb_ref, wout_ref,
         o_ref, h_s, u_s, sz_s, gt_s, v_s, wb_s, xp_s) = refs
    else:
        (mod_ref, g_ref, win_ref, dww_ref, dwb_ref, lng_ref, lnb_ref, wout_ref,
         o_ref, h_s, u_s, sz_s, gt_s, v_s, wb_s, xp_s) = refs
    d = D_MODEL
    seg = tile // N_SEG
    halo_rows = CONV_HALO * N_SEG
    main0 = halo_rows
    back0 = main0 + tile
    shift = mod_ref[0:1, :]
    gate = mod_ref[2:3, :]
    gs = g_ref[...] * (1.0 + mod_ref[1:2, :])

    def seg_rows(a):
        return pl.ds(a, N_SEG, stride=seg)

    def load_x(a):
        return jnp.concatenate([xr[seg_rows(a), :] for xr in x_refs], axis=-1)

    def pre(c, carry):
        a = c * (PRE_ROWS // N_SEG)
        xs = jnp.concatenate([load_x(a + j) for j in range(PRE_ROWS // N_SEG)], axis=0)
        rows = pl.ds(pl.multiple_of(c * PRE_ROWS, PRE_ROWS), PRE_ROWS)
        xp_s[rows, :] = xs
        h_s[rows, :] = _modulate(xs, gs, shift)
        return carry

    lax.fori_loop(0, tile // PRE_ROWS, pre, 0, unroll=8)
    n_rc = tile // ROW_CHUNK
    n_cc = d // COL_W
    for k in range(CONV_K):
        wb_s[k * N_SEG:(k + 1) * N_SEG, :] = jnp.broadcast_to(dww_ref[k:k + 1, :], (N_SEG, d))
    wb_s[CONV_K * N_SEG:(CONV_K + 1) * N_SEG, :] = jnp.broadcast_to(dwb_ref[...], (N_SEG, d))

    def row0(r):
        return r * ROW_CHUNK if isinstance(r, int) else pl.multiple_of(r * ROW_CHUNK, ROW_CHUNK)

    sub = lax.broadcasted_iota(jnp.int32, (N_SEG, 1), 0)
    if ext_halo:
        i = pl.program_id(0)
        h_s[tile:tile + CONV_HALO, :] = _modulate(xp_ref[...], gs, shift)
        h_s[tile + CONV_HALO:tile + 2 * CONV_HALO, :] = _modulate(xn_ref[...], gs, shift)
        he = h_s[tile:tile + 2 * CONV_HALO, :]
        for cb in range(d // COL_CHUNK):
            cs = cb * COL_CHUNK
            a = _dot(he, win_ref[:, cs:cs + COL_CHUNK])
            b = _dot(he, win_ref[:, d + cs:d + cs + COL_CHUNK])
            u_s[0:2 * CONV_HALO, cs:cs + COL_CHUNK] = a * _sigmoid(b)
        starts_seq = sub == 0
        ends_seq = sub == N_SEG - 1
    else:
        starts_seq = sub % segs_per_seq == 0
        ends_seq = sub % segs_per_seq == segs_per_seq - 1

    def fill_halo(j):
        cols = slice(j * COL_W, (j + 1) * COL_W)
        if ext_halo:
            u_prev = jnp.where(i % tiles_per_seq == 0, 0.0, u_s[0:CONV_HALO, cols])
            u_next = jnp.where(i % tiles_per_seq == tiles_per_seq - 1, 0.0, u_s[CONV_HALO:2 * CONV_HALO, cols])
        front = u_s[back0 - halo_rows - 1:back0 - 1, cols]
        back = u_s[main0 + 1:main0 + 1 + halo_rows, cols]
        for p in range(CONV_HALO):
            rows = slice(p * N_SEG, (p + 1) * N_SEG)
            edge_f = jnp.broadcast_to(u_prev[p:p + 1, :], (N_SEG, COL_W)) if ext_halo else 0.0
            edge_b = jnp.broadcast_to(u_next[p:p + 1, :], (N_SEG, COL_W)) if ext_halo else 0.0
            u_s[p * N_SEG:(p + 1) * N_SEG, cols] = jnp.where(starts_seq, edge_f, front[rows])
            u_s[back0 + p * N_SEG:back0 + (p + 1) * N_SEG, cols] = jnp.where(ends_seq, edge_b, back[rows])

    def conv_chunk(j, r):
        r0 = row0(r)
        for cb in range(j * COL_W // LANES, (j + 1) * COL_W // LANES):
            lanes = slice(cb * LANES, (cb + 1) * LANES)
            n_pos = ROW_CHUNK // N_SEG
            accs = [wb_s[CONV_K * N_SEG:(CONV_K + 1) * N_SEG, lanes]] * n_pos
            for k in range(CONV_K):
                wk = wb_s[k * N_SEG:(k + 1) * N_SEG, lanes]
                k0 = r0 + (k + CONV_HALO - CONV_K // 2) * N_SEG
                accs = [acc + u_s[pl.ds(k0 + p * N_SEG, N_SEG), lanes] * wk for p, acc in enumerate(accs)]
            v_s[pl.ds(r0, ROW_CHUNK), lanes] = jnp.concatenate(accs, axis=0)

    for cb in range(d // COL_CHUNK):
        cs = cb * COL_CHUNK
        hm = h_s[0:tile, :]
        a = _dot(hm, win_ref[:, cs:cs + COL_CHUNK])
        b = _dot(hm, win_ref[:, d + cs:d + cs + COL_CHUNK])
        u_s[main0:main0 + tile, cs:cs + COL_CHUNK] = a * _sigmoid(b)
        z = _dot(hm, win_ref[:, 2 * d + cs:2 * d + cs + COL_CHUNK])
        sz_s[:, cs:cs + COL_CHUNK] = _silu(z)
    for j in range(n_cc):
        fill_halo(j)

    def conv_rows(r, carry):
        for j in range(n_cc):
            conv_chunk(j, r)
        return carry

    def norm_gate(r):
        for part in range(ROW_CHUNK // NORM_ROWS):
            rows = pl.ds(row0(r) + part * NORM_ROWS, NORM_ROWS)
            acc = v_s[rows, :]
            mu = jnp.mean(acc, axis=-1, keepdims=True)
            dev = acc - mu
            var = jnp.mean(dev * dev, axis=-1, keepdims=True)
            y = dev * lax.rsqrt(var + EPS) * lng_ref[...] + lnb_ref[...]
            gt_s[rows, :] = (_silu(y) * sz_s[rows, :]).astype(BF16)

    def norm_rows(r, carry):
        norm_gate(r)
        return carry

    lax.fori_loop(0, n_rc, conv_rows, 0)
    lax.fori_loop(0, n_rc, norm_rows, 0)
    o_ref[...] = xp_s[...] + gate * _dot(gt_s[...], wout_ref[...])


def _mod_spec(row0, row_step, tiles_per_seq):
    return pl.BlockSpec((None, 3, D_MODEL), lambda i: (row0 + row_step * (i // tiles_per_seq), 0, 0))


def _resident(shape):
    return pl.BlockSpec(shape, lambda *_: (0,) * len(shape), pipeline_mode=pl.Buffered(1))


def _conv_layer(x, mod, row0, row_step, seq_len, g, w_in, dw_w, dw_b, ln_g, ln_b, w_out):
    n_tok, d = x.shape
    tile = CONV_TILE
    seg = SEG_LEN
    segs_per_seq = seq_len // seg
    ext_halo = segs_per_seq > N_SEG
    tiles_per_seq = max(segs_per_seq // N_SEG, 1)
    assert (segs_per_seq % N_SEG == 0) if ext_halo else (N_SEG % segs_per_seq == 0)
    hb = tile // CONV_HALO
    n_hblk = n_tok // CONV_HALO
    in_specs = [pl.BlockSpec((tile, LANES), lambda i, cb=cb: (i, cb)) for cb in range(N_COLB)]
    args = [x] * N_COLB
    if ext_halo:
        in_specs += [pl.BlockSpec((CONV_HALO, d), lambda i: (jnp.maximum(i * hb - 1, 0), 0)),
                     pl.BlockSpec((CONV_HALO, d), lambda i: (jnp.minimum((i + 1) * hb, n_hblk - 1), 0))]
        args += [x, x]
    in_specs += [_mod_spec(row0, row_step, tiles_per_seq),
                 _resident((1, d)), _resident((d, 3 * d)), _resident((CONV_K, d)),
                 _resident((1, d)), _resident((1, d)), _resident((1, d)), _resident((d, d))]
    args += [mod, g, w_in, dw_w, dw_b, ln_g, ln_b, w_out]
    h_rows = tile + 2 * CONV_HALO if ext_halo else tile
    return pl.pallas_call(
        functools.partial(_conv_body, tile=tile, ext_halo=ext_halo, tiles_per_seq=tiles_per_seq,
                          segs_per_seq=segs_per_seq),
        grid=(n_tok // tile,),
        in_specs=in_specs,
        out_specs=pl.BlockSpec((tile, d), lambda i: (i, 0)),
        out_shape=jax.ShapeDtypeStruct((n_tok, d), F32),
        scratch_shapes=[pltpu.VMEM((h_rows, d), BF16),
                        pltpu.VMEM((tile + 2 * CONV_HALO * N_SEG, d), F32),
                        pltpu.VMEM((tile, d), F32),
                        pltpu.VMEM((tile, d), BF16),
                        pltpu.VMEM((tile, d), F32),
                        pltpu.VMEM(((CONV_K + 1) * N_SEG, d), F32),
                        pltpu.VMEM((tile, d), F32)],
        compiler_params=_params(("arbitrary",)),
        name="conv_layer",
    )(*args).reshape(n_tok // tile, seg, N_SEG * d)


def _head_masks():
    lane = lax.broadcasted_iota(jnp.int32, (1, 2 * HEAD_DIM), 1)
    first = lane < HEAD_DIM
    return first, jnp.logical_not(first)


def _na_ctx_body(*refs, final, n_seg):
    x_refs, refs = refs[:n_seg], refs[n_seg:]
    mod_ref, g_ref, win_ref, wout_ref = refs[:4]
    fg_ref = refs[4] if final else None
    o_ref, k_ref, v_ref, q_s, kb_s, vb_s, sz_s, att_s = refs[-8:]
    d = D_MODEL
    x = _load_segments(x_refs)
    gs = g_ref[...] * (1.0 + mod_ref[1:2, :])
    h = _modulate(x, gs, mod_ref[0:1, :])
    for cb in range(d // COL_CHUNK):
        cs = cb * COL_CHUNK
        q_s[:, cs:cs + COL_CHUNK] = (_dot(h, win_ref[:, cs:cs + COL_CHUNK]) * (HEAD_DIM ** -0.5)).astype(BF16)
        k = _dot(h, win_ref[:, d + cs:d + cs + COL_CHUNK])
        k_ref[:, cs:cs + COL_CHUNK] = k
        kb_s[:, cs:cs + COL_CHUNK] = k.astype(BF16)
        v = _dot(h, win_ref[:, 2 * d + cs:2 * d + cs + COL_CHUNK])
        v_ref[:, cs:cs + COL_CHUNK] = v
        vb_s[:, cs:cs + COL_CHUNK] = v.astype(BF16)
        sz_s[:, cs:cs + COL_CHUNK] = _silu(_dot(h, win_ref[:, 3 * d + cs:3 * d + cs + COL_CHUNK]))

    masks = _head_masks()
    for hp in range(N_HEADS // 2):
        ls = hp * 2 * HEAD_DIM
        q = q_s[:, ls:ls + 2 * HEAD_DIM]
        kp = kb_s[:, ls:ls + 2 * HEAD_DIM]
        vp = vb_s[:, ls:ls + 2 * HEAD_DIM]
        outs = []
        for hh in range(2):
            s = _dot_nt(jnp.where(masks[hh], q, jnp.zeros_like(q)), kp)
            p = jnp.exp(s - jnp.max(s, axis=-1, keepdims=True))
            l = jnp.sum(p, axis=-1, keepdims=True)
            outs.append(_dot(p.astype(BF16), vp) / l)
        att = jnp.where(masks[0], outs[0], outs[1])
        att_s[:, ls:ls + 2 * HEAD_DIM] = (att * sz_s[:, ls:ls + 2 * HEAD_DIM]).astype(BF16)

    y = x + mod_ref[2:3, :] * _dot(att_s[...], wout_ref[...])
    if final:
        y = _rms(y) * fg_ref[...]
    o_ref[...] = y


def _na_ctx_layer(x, mod, g, w_in, w_out, seq_len, final_g=None):
    n_tok, d = x.size // D_MODEL, D_MODEL
    n_seq = n_tok // seq_len
    const = lambda b: (0, 0)
    final = final_g is not None
    x_specs = _segment_specs(seq_len)
    in_specs = x_specs + [pl.BlockSpec((None, 3, d), lambda b: (0, 0, 0)),
                          pl.BlockSpec((1, d), const),
                          pl.BlockSpec((d, 4 * d), const),
                          pl.BlockSpec((d, d), const)]
    args = [x] * len(x_specs) + [mod, g, w_in, w_out]
    if final:
        in_specs.append(pl.BlockSpec((1, d), const))
        args.append(final_g)
    seq_spec = pl.BlockSpec((seq_len, d), lambda b: (b, 0))
    return pl.pallas_call(
        functools.partial(_na_ctx_body, final=final, n_seg=len(x_specs)),
        grid=(n_seq,),
        in_specs=in_specs,
        out_specs=[seq_spec] * 3,
        out_shape=[jax.ShapeDtypeStruct((n_tok, d), F32)] * 3,
        scratch_shapes=[pltpu.VMEM((seq_len, d), BF16),
                        pltpu.VMEM((seq_len, d), BF16),
                        pltpu.VMEM((seq_len, d), BF16),
                        pltpu.VMEM((seq_len, d), F32),
                        pltpu.VMEM((seq_len, d), BF16)],
        compiler_params=_params(("arbitrary",)),
        name="na_ctx_layer",
    )(*args)


def _na_proj_body(*refs):
    mod_ref, g_ref, win_ref, q_ref, k_ref, v_ref, sz_ref = refs[-7:]
    d = D_MODEL
    gs = g_ref[...] * (1.0 + mod_ref[1:2, :])
    h = _modulate(_load_segments(refs[:-7]), gs, mod_ref[0:1, :])
    for cb in range(d // COL_CHUNK):
        cs = cb * COL_CHUNK
        q_ref[:, cs:cs + COL_CHUNK] = (_dot(h, win_ref[:, cs:cs + COL_CHUNK]) * (HEAD_DIM ** -0.5 * LOG2_E)).astype(BF16)
        k_ref[:, cs:cs + COL_CHUNK] = _dot(h, win_ref[:, d + cs:d + cs + COL_CHUNK]).astype(BF16)
        v_ref[:, cs:cs + COL_CHUNK] = _dot(h, win_ref[:, 2 * d + cs:2 * d + cs + COL_CHUNK]).astype(BF16)
        sz_ref[:, cs:cs + COL_CHUNK] = _silu(_dot(h, win_ref[:, 3 * d + cs:3 * d + cs + COL_CHUNK])).astype(BF16)


def _na_proj(x, mod, row0, seq_len, tile, g, w_in):
    n_tok, d = x.size // D_MODEL, D_MODEL
    tiles_per_seq = seq_len // tile
    const = lambda i: (0, 0)
    tok_spec = pl.BlockSpec((tile, d), lambda i: (i, 0))
    out = jax.ShapeDtypeStruct((n_tok, d), BF16)
    x_specs = _segment_specs(tile)
    return pl.pallas_call(
        _na_proj_body,
        grid=(n_tok // tile,),
        in_specs=x_specs + [_mod_spec(row0, 1, tiles_per_seq),
                            pl.BlockSpec((1, d), const),
                            pl.BlockSpec((d, 4 * d), const)],
        out_specs=[tok_spec] * 4,
        out_shape=[out] * 4,
        compiler_params=_params(("arbitrary",)),
        name="na_proj",
    )(*([x] * len(x_specs)), mod, g, w_in)


def _block_rows(qb):
    ks = min(max(Q_ROWS * qb - ROW_WIN // 2, 0), GRID_ROWS - K_ROWS)
    rs = [min(max(Q_ROWS * qb + qr - ROW_WIN // 2, 0), GRID_ROWS - ROW_WIN) for qr in range(Q_ROWS)]
    return ks, rs


_CLASS_BLOCKS = (0, 2, N_QBLK - 1)


def _na_att_body(q_ref, k_ref, v_ref, kc_ref, vc_ref, t2_ref, o_ref, bias_s, kc_s, vc_s):
    masks = _head_masks()
    first = masks[0]
    kc_s[...] = kc_ref[...].astype(BF16)

    def with_ones(v, hh):
        return jnp.where(masks[hh], v, jnp.ones_like(v))

    vc = vc_ref[...].astype(BF16)
    for hh in range(2):
        vc_s[hh] = with_ones(vc, hh)

    def tile_rows(qb, qr, m):
        ks, rs = _block_rows(qb)
        ka = ks + 2 * m
        va = rs[qr] <= ka < rs[qr] + ROW_WIN
        vb = rs[qr] <= ka + 1 < rs[qr] + ROW_WIN
        return va, vb, ka - (Q_ROWS * qb + qr) + ROW_WIN

    @pl.when(pl.program_id(1) == 0)
    def _():
        neg = jnp.full((GRID_W, 2 * GRID_W), NEG_INF, F32)
        for cls, qb in enumerate(_CLASS_BLOCKS):
            for hh in range(2):
                for qr in range(Q_ROWS):
                    for m in range(K_ROWS // 2):
                        va, vb, e = tile_rows(qb, qr, m)
                        if not (va or vb):
                            continue
                        t = t2_ref[hh, e]
                        if not va:
                            t = jnp.where(first, neg, t)
                        if not vb:
                            t = jnp.where(first, t, neg)
                        bias_s[cls * 2 + hh, qr * GRID_W:(qr + 1) * GRID_W,
                               m * 2 * GRID_W:(m + 1) * 2 * GRID_W] = t

    def block(q0, k0, cls):
        q = q_ref[pl.ds(q0, Q_BLK), :]
        kl = k_ref[pl.ds(k0, K_BLK), :]
        vl = v_ref[pl.ds(k0, K_BLK), :]
        qb = _CLASS_BLOCKS[cls]
        outs = []
        for hh in range(2):
            qm = jnp.where(masks[hh], q, jnp.zeros_like(q))
            sl = _dot_nt(qm, kl)
            sc = _dot_nt(qm, kc_s[...])
            p_l, p_c = [], []
            for qr in range(Q_ROWS):
                rows = slice(qr * GRID_W, (qr + 1) * GRID_W)
                tiles = {}
                for m in range(K_ROWS // 2):
                    va, vb, _ = tile_rows(qb, qr, m)
                    if va or vb:
                        cols = slice(m * 2 * GRID_W, (m + 1) * 2 * GRID_W)
                        tiles[m] = sl[rows, cols] + bias_s[cls * 2 + hh, rows, cols]
                s_ctx = sc[rows, :]
                mx = jnp.max(s_ctx, axis=-1, keepdims=True)
                for t in tiles.values():
                    mx = jnp.maximum(mx, jnp.max(t, axis=-1, keepdims=True))
                zero = jnp.zeros((GRID_W, 2 * GRID_W), BF16)
                p_l.append(jnp.concatenate(
                    [jnp.exp2(tiles[m] - mx).astype(BF16) if m in tiles else zero for m in range(K_ROWS // 2)],
                    axis=1))
                p_c.append(jnp.exp2(s_ctx - mx).astype(BF16))
            o = (_dot(jnp.concatenate(p_l, axis=0), with_ones(vl, hh))
                 + _dot(jnp.concatenate(p_c, axis=0), vc_s[hh]))
            outs.append(o / pltpu.roll(o, HEAD_DIM, 1))
        o_ref[pl.ds(q0, Q_BLK), :] = jnp.where(first, outs[0], outs[1]).astype(BF16)

    block(0, _block_rows(0)[0] * GRID_W, 0)

    def interior(qb, carry):
        q0 = pl.multiple_of(qb * Q_BLK, Q_BLK)
        block(q0, pl.multiple_of(q0 - (ROW_WIN // 2) * GRID_W, Q_BLK), 1)
        return carry

    lax.fori_loop(1, N_QBLK - 1, interior, 0, unroll=2)
    block((N_QBLK - 1) * Q_BLK, _block_rows(N_QBLK - 1)[0] * GRID_W, 2)


def _na_att(q, k, v, cache_k, cache_v, t2, layer_j, n_batch):
    n_tok, d = q.shape
    seq_len = n_tok // n_batch
    past = cache_k.shape[2]
    pw = 2 * HEAD_DIM
    tok_spec = pl.BlockSpec((seq_len, pw), lambda hp, b: (b, hp))
    cache_spec = pl.BlockSpec((None, None, past, pw), lambda hp, b: (b, layer_j, 0, hp))
    return pl.pallas_call(
        _na_att_body,
        grid=(N_HEADS // 2, n_batch),
        in_specs=[tok_spec, tok_spec, tok_spec, cache_spec, cache_spec,
                  pl.BlockSpec((None, 2, 2 * ROW_WIN, GRID_W, pw), lambda hp, b: (layer_j, hp, 0, 0, 0))],
        out_specs=tok_spec,
        out_shape=jax.ShapeDtypeStruct((n_tok, d), BF16),
        scratch_shapes=[pltpu.VMEM((3 * 2, Q_BLK, K_BLK), F32),
                        pltpu.VMEM((past, pw), BF16),
                        pltpu.VMEM((2, past, pw), BF16)],
        compiler_params=_params(("arbitrary", "arbitrary")),
        name="na_attention",
    )(q, k, v, cache_k, cache_v, t2)


def _na_out_body(*refs, final, n_seg):
    x_refs, refs = refs[:n_seg], refs[n_seg:]
    if final:
        att_ref, sz_ref, mod_ref, wout_ref, fg_ref, o_ref = refs
    else:
        att_ref, sz_ref, mod_ref, wout_ref, o_ref = refs
    a = (att_ref[...].astype(F32) * sz_ref[...].astype(F32)).astype(BF16)
    y = _load_segments(x_refs) + mod_ref[2:3, :] * _dot(a, wout_ref[...])
    if final:
        y = _rms(y) * fg_ref[...]
    o_ref[...] = y


def _na_out(x, att, sz, mod, row0, seq_len, tile, w_out, final_g=None):
    n_tok, d = x.size // D_MODEL, D_MODEL
    tiles_per_seq = seq_len // tile
    const = lambda i: (0, 0)
    tok_spec = pl.BlockSpec((tile, d), lambda i: (i, 0))
    final = final_g is not None
    x_specs = _segment_specs(tile)
    in_specs = x_specs + [tok_spec, tok_spec,
                          _mod_spec(row0, 1, tiles_per_seq),
                          pl.BlockSpec((d, d), const)]
    args = [x] * len(x_specs) + [att, sz, mod, w_out]
    if final:
        in_specs.append(pl.BlockSpec((1, d), const))
        args.append(final_g)
    return pl.pallas_call(
        functools.partial(_na_out_body, final=final, n_seg=len(x_specs)),
        grid=(n_tok // tile,),
        in_specs=in_specs,
        out_specs=tok_spec,
        out_shape=jax.ShapeDtypeStruct((n_tok, d), F32),
        compiler_params=_params(("arbitrary",)),
        name="na_out",
    )(*args)


def _bias_tables(rpb):
    qc = np.arange(GRID_W)[:, None]
    kc = np.arange(GRID_W)[None, :]
    start = np.clip(qc - COL_WIN // 2, 0, GRID_W - COL_WIN)
    in_win = (kc >= start) & (kc < start + COL_WIN)
    dc = kc - qc + COL_WIN - 1
    onehot = (np.arange(2 * COL_WIN - 1)[:, None, None] == dc[None]) & in_win[None]
    c = jnp.einsum("lhdj,jqk->lhdqk", rpb, jnp.asarray(onehot, F32), precision=lax.Precision.HIGHEST)
    c = jnp.where(jnp.asarray(in_win), c * LOG2_E, NEG_INF)
    neg = jnp.full(c.shape[:2] + (1, GRID_W, GRID_W), NEG_INF, F32)
    cx = jnp.concatenate([neg, c, neg], axis=2)
    return jnp.concatenate([cx[:, :, :2 * ROW_WIN], cx[:, :, 1:2 * ROW_WIN + 1]], axis=-1)


NA_TILE = 512


def kernel(x_prompt, x_sample, c, cache_k, cache_v, c_ctx, norm_g, ada_w, ada_b, conv_w_in, conv_dw_w,
           conv_dw_b, conv_ln_g, conv_ln_b, conv_w_out, na_w_in, na_rpb, na_w_out, final_g):
    n_ctx, ctx_len, d = x_prompt.shape
    n_dec, dec_len, _ = x_sample.shape
    n_na = na_w_in.shape[0]
    past = cache_k.shape[2]

    cvec = jnp.concatenate([c_ctx[None], c, jnp.zeros((MOD_ROWS - 1 - n_dec, d), F32)], axis=0)
    mod = _ada(cvec, ada_w, ada_b).reshape(DEPTH, MOD_ROWS, 3, d)

    xc = x_prompt.reshape(n_ctx * ctx_len, d)
    xs = x_sample.reshape(n_dec * dec_len, d)
    ck = cache_k.reshape(n_dec, n_na, past, d)
    cv = cache_v.reshape(n_dec, n_na, past, d)
    fg = final_g.reshape(1, d)
    bias_tables = _bias_tables(na_rpb)
    new_k, new_v = [], []
    for i in range(DEPTH):
        j = i // 2
        g = norm_g[i].reshape(1, d)
        if i % 2 == 0:
            wi = conv_w_in[j].astype(BF16)
            wo = conv_w_out[j].astype(BF16)
            rest = (g, wi, conv_dw_w[j], conv_dw_b[j].reshape(1, d), conv_ln_g[j].reshape(1, d),
                    conv_ln_b[j].reshape(1, d), wo)
            xc = _conv_layer(xc, mod[i], 0, 0, ctx_len, *rest)
            xs = _conv_layer(xs, mod[i], 1, 1, dec_len, *rest)
        else:
            wi = na_w_in[j].astype(BF16)
            wo = na_w_out[j].astype(BF16)
            last = i == DEPTH - 1
            xc, k_ctx, v_ctx = _na_ctx_layer(xc, mod[i], g, wi, wo, ctx_len, final_g=fg if last else None)
            new_k.append(k_ctx.reshape(n_ctx, ctx_len, N_HEADS, HEAD_DIM))
            new_v.append(v_ctx.reshape(n_ctx, ctx_len, N_HEADS, HEAD_DIM))
            q, k, v, sz = _na_proj(xs, mod[i], 1, dec_len, NA_TILE, g, wi)
            att = _na_att(q, k, v, ck, cv, bias_tables, j, n_dec)
            xs = _na_out(xs, att, sz, mod[i], 1, dec_len, NA_TILE, wo, final_g=fg if last else None)

    y_prompt = xc.reshape(n_ctx, ctx_len, d)
    y_sample = xs.reshape(n_dec, dec_len, d)
    return (y_prompt, y_sample, jnp.stack(new_k, axis=1), jnp.stack(new_v, axis=1))
```

```python
import functools

import numpy as np
import jax
import jax.numpy as jnp
from jax import lax
from jax.experimental import pallas as pl
from jax.experimental.pallas import tpu as pltpu

F32 = jnp.float32
BF16 = jnp.bfloat16

D_MODEL = 1024
LANES = 128
N_COLB = D_MODEL // LANES
DEPTH = 4
N_HEADS = 16
HEAD_DIM = 64
GRID_W = 64
GRID_ROWS = 32
CONV_K = 31
CONV_HALO = 16
ROW_WIN = 8
COL_WIN = 16
EPS = 1e-6
NEG_INF = -1e30
LOG2_E = 1.4426950408889634
MOD_ROWS = 16

Q_ROWS = 4
Q_BLK = Q_ROWS * GRID_W
K_ROWS = 12
K_BLK = K_ROWS * GRID_W
N_QBLK = GRID_ROWS // Q_ROWS

VMEM_LIMIT_BYTES = 56 * 1024 * 1024


def _params(sem):
    return pltpu.CompilerParams(dimension_semantics=sem, vmem_limit_bytes=VMEM_LIMIT_BYTES)


def _sigmoid(x):
    return 1.0 / (1.0 + jnp.exp(-x))


def _silu(x):
    return x * _sigmoid(x)


def _rms(x):
    return x * lax.rsqrt(jnp.mean(x * x, axis=-1, keepdims=True) + EPS)


def _dot(a, b):
    return jnp.dot(a, b, preferred_element_type=F32)


def _dot_nt(a, b):
    return lax.dot_general(a, b, (((1,), (1,)), ((), ())), preferred_element_type=F32)


ADA_TILE = 768


def _ada_body(c_ref, w_ref, b_ref, o_ref):
    s = _silu(c_ref[...]).astype(BF16)
    o_ref[...] = _dot(s, w_ref[...].astype(BF16)) + b_ref[...]


def _ada(cvec, ada_w, ada_b):
    n_out = 3 * D_MODEL
    return pl.pallas_call(
        _ada_body,
        grid=(DEPTH, n_out // ADA_TILE),
        in_specs=[pl.BlockSpec((MOD_ROWS, D_MODEL), lambda l, n: (0, 0)),
                  pl.BlockSpec((None, D_MODEL, ADA_TILE), lambda l, n: (l, 0, n)),
                  pl.BlockSpec((None, 1, ADA_TILE), lambda l, n: (l, 0, n))],
        out_specs=pl.BlockSpec((None, MOD_ROWS, ADA_TILE), lambda l, n: (l, 0, n)),
        out_shape=jax.ShapeDtypeStruct((DEPTH, MOD_ROWS, n_out), F32),
        compiler_params=_params(("arbitrary", "arbitrary")),
        name="ada_params",
    )(cvec, ada_w, ada_b.reshape(DEPTH, 1, n_out))


def _modulate(x, gs, shift):
    return (_rms(x) * gs + shift).astype(BF16)


N_SEG = 8
CONV_TILE = 1024
SEG_LEN = CONV_TILE // N_SEG
PRE_ROWS = 16
ROW_CHUNK = 128
COL_W = 256
NORM_ROWS = 32
COL_CHUNK = 512


def _colblock_spec(rows):
    return pl.BlockSpec((N_COLB, rows, LANES), lambda i: (0, i, 0))


def _load_colblocks(x_ref):
    return jnp.concatenate([x_ref[cb] for cb in range(N_COLB)], axis=-1)


def _conv_body(*refs, tile, ext_halo, tiles_per_seq, segs_per_seq):
    x_refs, refs = refs[:N_COLB], refs[N_COLB:]
    if ext_halo:
        (xp_ref, xn_ref, mod_ref, g_ref, win_ref, dww_ref, dwb_ref, lng_ref, lnb_ref, wout_ref,
         o_ref, h_s, u_s, sz_s, gt_s, v_s, wb_s, xp_s) = refs
    else:
        (mod_ref, g_ref, win_ref, dww_ref, dwb_ref, lng_ref, lnb_ref, wout_ref,
         o_ref, h_s, u_s, sz_s, gt_s, v_s, wb_s, xp_s) = refs
    d = D_MODEL
    seg = tile // N_SEG
    halo_rows = CONV_HALO * N_SEG
    main0 = halo_rows
    back0 = main0 + tile
    shift = mod_ref[0:1, :]
    gate = mod_ref[2:3, :]
    gs = g_ref[...] * (1.0 + mod_ref[1:2, :])

    def seg_rows(a):
        return pl.ds(a, N_SEG, stride=seg)

    def load_x(a):
        return jnp.concatenate([xr[seg_rows(a), :] for xr in x_refs], axis=-1)

    def pre(c, carry):
        a = c * (PRE_ROWS // N_SEG)
        xs = jnp.concatenate([load_x(a + j) for j in range(PRE_ROWS // N_SEG)], axis=0)
        rows = pl.ds(pl.multiple_of(c * PRE_ROWS, PRE_ROWS), PRE_ROWS)
        xp_s[rows, :] = xs
        h_s[rows, :] = _modulate(xs, gs, shift)
        return carry

    lax.fori_loop(0, tile // PRE_ROWS, pre, 0, unroll=8)
    n_rc = tile // ROW_CHUNK
    n_cc = d // COL_W
    for k in range(CONV_K):
        wb_s[k * N_SEG:(k + 1) * N_SEG, :] = jnp.broadcast_to(dww_ref[k:k + 1, :], (N_SEG, d))
    wb_s[CONV_K * N_SEG:(CONV_K + 1) * N_SEG, :] = jnp.broadcast_to(dwb_ref[...], (N_SEG, d))

    def row0(r):
        return r * ROW_CHUNK if isinstance(r, int) else pl.multiple_of(r * ROW_CHUNK, ROW_CHUNK)

    sub = lax.broadcasted_iota(jnp.int32, (N_SEG, 1), 0)
    if ext_halo:
        i = pl.program_id(0)
        h_s[tile:tile + CONV_HALO, :] = _modulate(xp_ref[...], gs, shift)
        h_s[tile + CONV_HALO:tile + 2 * CONV_HALO, :] = _modulate(xn_ref[...], gs, shift)
        he = h_s[tile:tile + 2 * CONV_HALO, :]
        for cb in range(d // COL_CHUNK):
            cs = cb * COL_CHUNK
            a = _dot(he, win_ref[:, cs:cs + COL_CHUNK])
            b = _dot(he, win_ref[:, d + cs:d + cs + COL_CHUNK])
            u_s[0:2 * CONV_HALO, cs:cs + COL_CHUNK] = a * _sigmoid(b)
        starts_seq = sub == 0
        ends_seq = sub == N_SEG - 1
    else:
        starts_seq = sub % segs_per_seq == 0
        ends_seq = sub % segs_per_seq == segs_per_seq - 1

    def fill_halo(j):
        cols = slice(j * COL_W, (j + 1) * COL_W)
        if ext_halo:
            u_prev = jnp.where(i % tiles_per_seq == 0, 0.0, u_s[0:CONV_HALO, cols])
            u_next = jnp.where(i % tiles_per_seq == tiles_per_seq - 1, 0.0, u_s[CONV_HALO:2 * CONV_HALO, cols])
        front = u_s[back0 - halo_rows - 1:back0 - 1, cols]
        back = u_s[main0 + 1:main0 + 1 + halo_rows, cols]
        for p in range(CONV_HALO):
            rows = slice(p * N_SEG, (p + 1) * N_SEG)
            edge_f = jnp.broadcast_to(u_prev[p:p + 1, :], (N_SEG, COL_W)) if ext_halo else 0.0
            edge_b = jnp.broadcast_to(u_next[p:p + 1, :], (N_SEG, COL_W)) if ext_halo else 0.0
            u_s[p * N_SEG:(p + 1) * N_SEG, cols] = jnp.where(starts_seq, edge_f, front[rows])
            u_s[back0 + p * N_SEG:back0 + (p + 1) * N_SEG, cols] = jnp.where(ends_seq, edge_b, back[rows])

    def conv_chunk(j, r):
        r0 = row0(r)
        for cb in range(j * COL_W // LANES, (j + 1) * COL_W // LANES):
            lanes = slice(cb * LANES, (cb + 1) * LANES)
            n_pos = ROW_CHUNK // N_SEG
            accs = [wb_s[CONV_K * N_SEG:(CONV_K + 1) * N_SEG, lanes]] * n_pos
            for k in range(CONV_K):
                wk = wb_s[k * N_SEG:(k + 1) * N_SEG, lanes]
                k0 = r0 + (k + CONV_HALO - CONV_K // 2) * N_SEG
                accs = [acc + u_s[pl.ds(k0 + p * N_SEG, N_SEG), lanes] * wk for p, acc in enumerate(accs)]
            v_s[pl.ds(r0, ROW_CHUNK), lanes] = jnp.concatenate(accs, axis=0)

    for cb in range(d // COL_CHUNK):
        cs = cb * COL_CHUNK
        hm = h_s[0:tile, :]
        a = _dot(hm, win_ref[:, cs:cs + COL_CHUNK])
        b = _dot(hm, win_ref[:, d + cs:d + cs + COL_CHUNK])
        u_s[main0:main0 + tile, cs:cs + COL_CHUNK] = a * _sigmoid(b)
        z = _dot(hm, win_ref[:, 2 * d + cs:2 * d + cs + COL_CHUNK])
        sz_s[:, cs:cs + COL_CHUNK] = _silu(z)
    for j in range(n_cc):
        fill_halo(j)

    def conv_rows(r, carry):
        for j in range(n_cc):
            conv_chunk(j, r)
        return carry

    def norm_gate(r):
        for part in range(ROW_CHUNK // NORM_ROWS):
            rows = pl.ds(row0(r) + part * NORM_ROWS, NORM_ROWS)
            acc = v_s[rows, :]
            mu = jnp.mean(acc, axis=-1, keepdims=True)
            dev = acc - mu
            var = jnp.mean(dev * dev, axis=-1, keepdims=True)
            y = dev * lax.rsqrt(var + EPS) * lng_ref[...] + lnb_ref[...]
            gt_s[rows, :] = (_silu(y) * sz_s[rows, :]).astype(BF16)

    def norm_rows(r, carry):
        norm_gate(r)
        return carry

    lax.fori_loop(0, n_rc, conv_rows, 0)
    lax.fori_loop(0, n_rc, norm_rows, 0)
    xp_s[...] = xp_s[...] + gate * _dot(gt_s[...], wout_ref[...])

    def post(a, carry):
        y = xp_s[pl.ds(pl.multiple_of(a * N_SEG, N_SEG), N_SEG), :]
        for cb in range(N_COLB):
            o_ref[cb, seg_rows(a), :] = y[:, cb * LANES:(cb + 1) * LANES]
        return carry

    lax.fori_loop(0, seg, post, 0, unroll=16)


def _mod_spec(row0, row_step, tiles_per_seq):
    return pl.BlockSpec((None, 3, D_MODEL), lambda i: (row0 + row_step * (i // tiles_per_seq), 0, 0))


def _resident(shape):
    return pl.BlockSpec(shape, lambda *_: (0,) * len(shape), pipeline_mode=pl.Buffered(1))


def _conv_layer(x, mod, row0, row_step, seq_len, g, w_in, dw_w, dw_b, ln_g, ln_b, w_out):
    n_tok, d = x.shape
    tile = CONV_TILE
    seg = SEG_LEN
    segs_per_seq = seq_len // seg
    ext_halo = segs_per_seq > N_SEG
    tiles_per_seq = max(segs_per_seq // N_SEG, 1)
    assert (segs_per_seq % N_SEG == 0) if ext_halo else (N_SEG % segs_per_seq == 0)
    hb = tile // CONV_HALO
    n_hblk = n_tok // CONV_HALO
    in_specs = [pl.BlockSpec((tile, LANES), lambda i, cb=cb: (i, cb)) for cb in range(N_COLB)]
    args = [x] * N_COLB
    if ext_halo:
        in_specs += [pl.BlockSpec((CONV_HALO, d), lambda i: (jnp.maximum(i * hb - 1, 0), 0)),
                     pl.BlockSpec((CONV_HALO, d), lambda i: (jnp.minimum((i + 1) * hb, n_hblk - 1), 0))]
        args += [x, x]
    in_specs += [_mod_spec(row0, row_step, tiles_per_seq),
                 _resident((1, d)), _resident((d, 3 * d)), _resident((CONV_K, d)),
                 _resident((1, d)), _resident((1, d)), _resident((1, d)), _resident((d, d))]
    args += [mod, g, w_in, dw_w, dw_b, ln_g, ln_b, w_out]
    h_rows = tile + 2 * CONV_HALO if ext_halo else tile
    return pl.pallas_call(
        functools.partial(_conv_body, tile=tile, ext_halo=ext_halo, tiles_per_seq=tiles_per_seq,
                          segs_per_seq=segs_per_seq),
        grid=(n_tok // tile,),
        in_specs=in_specs,
        out_specs=_colblock_spec(tile),
        out_shape=jax.ShapeDtypeStruct((N_COLB, n_tok, LANES), F32),
        scratch_shapes=[pltpu.VMEM((h_rows, d), BF16),
                        pltpu.VMEM((tile + 2 * CONV_HALO * N_SEG, d), F32),
                        pltpu.VMEM((tile, d), F32),
                        pltpu.VMEM((tile, d), BF16),
                        pltpu.VMEM((tile, d), F32),
                        pltpu.VMEM(((CONV_K + 1) * N_SEG, d), F32),
                        pltpu.VMEM((tile, d), F32)],
        compiler_params=_params(("arbitrary",)),
        name="conv_layer",
    )(*args)


def _head_masks():
    lane = lax.broadcasted_iota(jnp.int32, (1, 2 * HEAD_DIM), 1)
    first = lane < HEAD_DIM
    return first, jnp.logical_not(first)


def _na_ctx_body(*refs, final, n_prev):
    x_ref, mod_ref, g_ref, win_ref, wout_ref = refs[:5]
    fg_ref = refs[5] if final else None
    o_ref, k_ref, v_ref, q_s, kb_s, vb_s, sz_s, att_s = refs[-8:]
    d = D_MODEL
    if n_prev:
        kprev_ref, vprev_ref = refs[-10:-8]
        k_ref[0:n_prev] = kprev_ref[...]
        v_ref[0:n_prev] = vprev_ref[...]
    x = _load_colblocks(x_ref)
    gs = g_ref[...] * (1.0 + mod_ref[1:2, :])
    h = _modulate(x, gs, mod_ref[0:1, :])
    for cb in range(d // COL_CHUNK):
        cs = cb * COL_CHUNK
        q_s[:, cs:cs + COL_CHUNK] = (_dot(h, win_ref[:, cs:cs + COL_CHUNK]) * (HEAD_DIM ** -0.5)).astype(BF16)
        k = _dot(h, win_ref[:, d + cs:d + cs + COL_CHUNK])
        k_ref[n_prev, :, cs:cs + COL_CHUNK] = k
        kb_s[:, cs:cs + COL_CHUNK] = k.astype(BF16)
        v = _dot(h, win_ref[:, 2 * d + cs:2 * d + cs + COL_CHUNK])
        v_ref[n_prev, :, cs:cs + COL_CHUNK] = v
        vb_s[:, cs:cs + COL_CHUNK] = v.astype(BF16)
        sz_s[:, cs:cs + COL_CHUNK] = _silu(_dot(h, win_ref[:, 3 * d + cs:3 * d + cs + COL_CHUNK]))

    masks = _head_masks()
    for hp in range(N_HEADS // 2):
        ls = hp * 2 * HEAD_DIM
        q = q_s[:, ls:ls + 2 * HEAD_DIM]
        kp = kb_s[:, ls:ls + 2 * HEAD_DIM]
        vp = vb_s[:, ls:ls + 2 * HEAD_DIM]
        outs = []
        for hh in range(2):
            s = _dot_nt(jnp.where(masks[hh], q, jnp.zeros_like(q)), kp)
            p = jnp.exp(s - jnp.max(s, axis=-1, keepdims=True))
            l = jnp.sum(p, axis=-1, keepdims=True)
            outs.append(_dot(p.astype(BF16), vp) / l)
        att = jnp.where(masks[0], outs[0], outs[1])
        att_s[:, ls:ls + 2 * HEAD_DIM] = (att * sz_s[:, ls:ls + 2 * HEAD_DIM]).astype(BF16)

    y = x + mod_ref[2:3, :] * _dot(att_s[...], wout_ref[...])
    if final:
        y = _rms(y) * fg_ref[...]
    o_ref[...] = y


def _na_ctx_layer(x, mod, g, w_in, w_out, seq_len, final_g=None, prev=None):
    n_tok, d = x.shape[1], D_MODEL
    n_seq = n_tok // seq_len
    n_prev = 0 if prev is None else prev[0].shape[1]
    const = lambda b: (0, 0)
    final = final_g is not None
    in_specs = [_colblock_spec(seq_len),
                pl.BlockSpec((None, 3, d), lambda b: (0, 0, 0)),
                pl.BlockSpec((1, d), const),
                pl.BlockSpec((d, 4 * d), const),
                pl.BlockSpec((d, d), const)]
    args = [x, mod, g, w_in, w_out]
    if final:
        in_specs.append(pl.BlockSpec((1, d), const))
        args.append(final_g)
    if n_prev:
        in_specs += [pl.BlockSpec((None, n_prev, seq_len, d), lambda b: (b, 0, 0, 0))] * 2
        args += list(prev)
    stack_spec = pl.BlockSpec((None, n_prev + 1, seq_len, d), lambda b: (b, 0, 0, 0))
    stack_shape = jax.ShapeDtypeStruct((n_seq, n_prev + 1, seq_len, d), F32)
    return pl.pallas_call(
        functools.partial(_na_ctx_body, final=final, n_prev=n_prev),
        grid=(n_seq,),
        in_specs=in_specs,
        out_specs=[pl.BlockSpec((seq_len, d), lambda b: (b, 0)), stack_spec, stack_spec],
        out_shape=[jax.ShapeDtypeStruct((n_tok, d), F32), stack_shape, stack_shape],
        scratch_shapes=[pltpu.VMEM((seq_len, d), BF16),
                        pltpu.VMEM((seq_len, d), BF16),
                        pltpu.VMEM((seq_len, d), BF16),
                        pltpu.VMEM((seq_len, d), F32),
                        pltpu.VMEM((seq_len, d), BF16)],
        compiler_params=_params(("arbitrary",)),
        name="na_ctx_layer",
    )(*args)


def _na_proj_body(x_ref, mod_ref, g_ref, win_ref, q_ref, k_ref, v_ref, sz_ref):
    d = D_MODEL
    gs = g_ref[...] * (1.0 + mod_ref[1:2, :])
    h = _modulate(_load_colblocks(x_ref), gs, mod_ref[0:1, :])
    for cb in range(d // COL_CHUNK):
        cs = cb * COL_CHUNK
        q_ref[:, cs:cs + COL_CHUNK] = (_dot(h, win_ref[:, cs:cs + COL_CHUNK]) * (HEAD_DIM ** -0.5 * LOG2_E)).astype(BF16)
        k_ref[:, cs:cs + COL_CHUNK] = _dot(h, win_ref[:, d + cs:d + cs + COL_CHUNK]).astype(BF16)
        v_ref[:, cs:cs + COL_CHUNK] = _dot(h, win_ref[:, 2 * d + cs:2 * d + cs + COL_CHUNK]).astype(BF16)
        sz_ref[:, cs:cs + COL_CHUNK] = _silu(_dot(h, win_ref[:, 3 * d + cs:3 * d + cs + COL_CHUNK])).astype(BF16)


def _na_proj(x, mod, row0, seq_len, tile, g, w_in):
    n_tok, d = x.shape[1], D_MODEL
    tiles_per_seq = seq_len // tile
    const = lambda i: (0, 0)
    tok_spec = pl.BlockSpec((tile, d), lambda i: (i, 0))
    out = jax.ShapeDtypeStruct((n_tok, d), BF16)
    return pl.pallas_call(
        _na_proj_body,
        grid=(n_tok // tile,),
        in_specs=[_colblock_spec(tile),
                  _mod_spec(row0, 1, tiles_per_seq),
                  pl.BlockSpec((1, d), const),
                  pl.BlockSpec((d, 4 * d), const)],
        out_specs=[tok_spec] * 4,
        out_shape=[out] * 4,
        compiler_params=_params(("arbitrary",)),
        name="na_proj",
    )(x, mod, g, w_in)


def _block_rows(qb):
    ks = min(max(Q_ROWS * qb - ROW_WIN // 2, 0), GRID_ROWS - K_ROWS)
    rs = [min(max(Q_ROWS * qb + qr - ROW_WIN // 2, 0), GRID_ROWS - ROW_WIN) for qr in range(Q_ROWS)]
    return ks, rs


_CLASS_BLOCKS = (0, 2, N_QBLK - 1)


def _na_att_body(q_ref, k_ref, v_ref, kc_ref, vc_ref, t2_ref, o_ref, bias_s, kc_s, vc_s):
    masks = _head_masks()
    first = masks[0]
    kc_s[...] = kc_ref[...].astype(BF16)

    def with_ones(v, hh):
        return jnp.where(masks[hh], v, jnp.ones_like(v))

    vc = vc_ref[...].astype(BF16)
    for hh in range(2):
        vc_s[hh] = with_ones(vc, hh)

    def tile_rows(qb, qr, m):
        ks, rs = _block_rows(qb)
        ka = ks + 2 * m
        va = rs[qr] <= ka < rs[qr] + ROW_WIN
        vb = rs[qr] <= ka + 1 < rs[qr] + ROW_WIN
        return va, vb, ka - (Q_ROWS * qb + qr) + ROW_WIN

    @pl.when(pl.program_id(1) == 0)
    def _():
        neg = jnp.full((GRID_W, 2 * GRID_W), NEG_INF, F32)
        for cls, qb in enumerate(_CLASS_BLOCKS):
            for hh in range(2):
                for qr in range(Q_ROWS):
                    for m in range(K_ROWS // 2):
                        va, vb, e = tile_rows(qb, qr, m)
                        if not (va or vb):
                            continue
                        t = t2_ref[hh, e]
                        if not va:
                            t = jnp.where(first, neg, t)
                        if not vb:
                            t = jnp.where(first, t, neg)
                        bias_s[cls * 2 + hh, qr * GRID_W:(qr + 1) * GRID_W,
                               m * 2 * GRID_W:(m + 1) * 2 * GRID_W] = t

    def block(q0, k0, cls):
        q = q_ref[pl.ds(q0, Q_BLK), :]
        kl = k_ref[pl.ds(k0, K_BLK), :]
        vl = v_ref[pl.ds(k0, K_BLK), :]
        qb = _CLASS_BLOCKS[cls]
        outs = []
        for hh in range(2):
            qm = jnp.where(masks[hh], q, jnp.zeros_like(q))
            sl = _dot_nt(qm, kl)
            sc = _dot_nt(qm, kc_s[...])
            p_l, p_c = [], []
            for qr in range(Q_ROWS):
                rows = slice(qr * GRID_W, (qr + 1) * GRID_W)
                tiles = {}
                for m in range(K_ROWS // 2):
                    va, vb, _ = tile_rows(qb, qr, m)
                    if va or vb:
                        cols = slice(m * 2 * GRID_W, (m + 1) * 2 * GRID_W)
                        tiles[m] = sl[rows, cols] + bias_s[cls * 2 + hh, rows, cols]
                s_ctx = sc[rows, :]
                mx = jnp.max(s_ctx, axis=-1, keepdims=True)
                for t in tiles.values():
                    mx = jnp.maximum(mx, jnp.max(t, axis=-1, keepdims=True))
                zero = jnp.zeros((GRID_W, 2 * GRID_W), BF16)
                p_l.append(jnp.concatenate(
                    [jnp.exp2(tiles[m] - mx).astype(BF16) if m in tiles else zero for m in range(K_ROWS // 2)],
                    axis=1))
                p_c.append(jnp.exp2(s_ctx - mx).astype(BF16))
            o = (_dot(jnp.concatenate(p_l, axis=0), with_ones(vl, hh))
                 + _dot(jnp.concatenate(p_c, axis=0), vc_s[hh]))
            outs.append(o / pltpu.roll(o, HEAD_DIM, 1))
        o_ref[pl.ds(q0, Q_BLK), :] = jnp.where(first, outs[0], outs[1]).astype(BF16)

    block(0, _block_rows(0)[0] * GRID_W, 0)
    block((N_QBLK - 1) * Q_BLK, _block_rows(N_QBLK - 1)[0] * GRID_W, 2)

    def interior(qb, carry):
        q0 = pl.multiple_of(qb * Q_BLK, Q_BLK)
        block(q0, pl.multiple_of(q0 - (ROW_WIN // 2) * GRID_W, Q_BLK), 1)
        return carry

    lax.fori_loop(1, N_QBLK - 1, interior, 0, unroll=2)


def _na_att(q, k, v, cache_k, cache_v, t2, layer_j, n_batch):
    n_tok, d = q.shape
    seq_len = n_tok // n_batch
    past = cache_k.shape[2]
    pw = 2 * HEAD_DIM
    tok_spec = pl.BlockSpec((seq_len, pw), lambda hp, b: (b, hp))
    cache_spec = pl.BlockSpec((None, None, past, pw), lambda hp, b: (b, layer_j, 0, hp))
    return pl.pallas_call(
        _na_att_body,
        grid=(N_HEADS // 2, n_batch),
        in_specs=[tok_spec, tok_spec, tok_spec, cache_spec, cache_spec,
                  pl.BlockSpec((None, 2, 2 * ROW_WIN, GRID_W, pw), lambda hp, b: (layer_j, hp, 0, 0, 0))],
        out_specs=tok_spec,
        out_shape=jax.ShapeDtypeStruct((n_tok, d), BF16),
        scratch_shapes=[pltpu.VMEM((3 * 2, Q_BLK, K_BLK), F32),
                        pltpu.VMEM((past, pw), BF16),
                        pltpu.VMEM((2, past, pw), BF16)],
        compiler_params=_params(("arbitrary", "arbitrary")),
        name="na_attention",
    )(q, k, v, cache_k, cache_v, t2)


def _na_out_body(*refs, final):
    if final:
        x_ref, att_ref, sz_ref, mod_ref, wout_ref, fg_ref, o_ref = refs
    else:
        x_ref, att_ref, sz_ref, mod_ref, wout_ref, o_ref = refs
    a = (att_ref[...].astype(F32) * sz_ref[...].astype(F32)).astype(BF16)
    y = _load_colblocks(x_ref) + mod_ref[2:3, :] * _dot(a, wout_ref[...])
    if final:
        y = _rms(y) * fg_ref[...]
    o_ref[...] = y


def _na_out(x, att, sz, mod, row0, seq_len, tile, w_out, final_g=None):
    n_tok, d = x.shape[1], D_MODEL
    tiles_per_seq = seq_len // tile
    const = lambda i: (0, 0)
    tok_spec = pl.BlockSpec((tile, d), lambda i: (i, 0))
    final = final_g is not None
    in_specs = [_colblock_spec(tile), tok_spec, tok_spec,
                _mod_spec(row0, 1, tiles_per_seq),
                pl.BlockSpec((d, d), const)]
    args = [x, att, sz, mod, w_out]
    if final:
        in_specs.append(pl.BlockSpec((1, d), const))
        args.append(final_g)
    return pl.pallas_call(
        functools.partial(_na_out_body, final=final),
        grid=(n_tok // tile,),
        in_specs=in_specs,
        out_specs=tok_spec,
        out_shape=jax.ShapeDtypeStruct((n_tok, d), F32),
        compiler_params=_params(("arbitrary",)),
        name="na_out",
    )(*args)


def _bias_tables(rpb):
    n_dr, n_dc = 2 * ROW_WIN - 1, 2 * COL_WIN - 1
    qc = np.arange(GRID_W)[:, None]
    lane = np.arange(2 * GRID_W)[None, :]
    kc, half = lane % GRID_W, lane // GRID_W
    start = np.clip(qc - COL_WIN // 2, 0, GRID_W - COL_WIN)
    in_win = (kc >= start) & (kc < start + COL_WIN)
    dc = kc - qc + COL_WIN - 1
    onehot = ((np.arange(n_dc)[:, None, None] == dc[None]) & in_win[None])[None] \
        & (np.arange(2)[:, None, None, None] == half[None, None])
    padded = jnp.pad(rpb, ((0, 0), (0, 0), (1, 1), (0, 0)))
    rows = jnp.stack([padded[:, :, :n_dr + 1], padded[:, :, 1:n_dr + 2]], axis=3)
    t = jnp.einsum("lhexj,xjqk->lheqk", rows, jnp.asarray(onehot, F32), precision=lax.Precision.HIGHEST)
    dr = np.arange(n_dr + 1)[:, None, None] - 1 + half[None]
    valid = (dr >= 0) & (dr < n_dr) & in_win[None]
    return jnp.where(jnp.asarray(valid), t * LOG2_E, NEG_INF)


NA_TILE = 512


def kernel(x_prompt, x_sample, c, cache_k, cache_v, c_ctx, norm_g, ada_w, ada_b, conv_w_in, conv_dw_w,
           conv_dw_b, conv_ln_g, conv_ln_b, conv_w_out, na_w_in, na_rpb, na_w_out, final_g):
    n_ctx, ctx_len, d = x_prompt.shape
    n_dec, dec_len, _ = x_sample.shape
    n_na = na_w_in.shape[0]
    past = cache_k.shape[2]

    cvec = jnp.concatenate([c_ctx[None], c, jnp.zeros((MOD_ROWS - 1 - n_dec, d), F32)], axis=0)
    mod = _ada(cvec, ada_w, ada_b).reshape(DEPTH, MOD_ROWS, 3, d)

    xc = x_prompt.reshape(n_ctx * ctx_len, d)
    xs = x_sample.reshape(n_dec * dec_len, d)
    ck = cache_k.reshape(n_dec, n_na, past, d)
    cv = cache_v.reshape(n_dec, n_na, past, d)
    fg = final_g.reshape(1, d)
    bias_tables = _bias_tables(na_rpb)
    ctx_kv = None
    for i in range(DEPTH):
        j = i // 2
        g = norm_g[i].reshape(1, d)
        if i % 2 == 0:
            wi = conv_w_in[j].astype(BF16)
            wo = conv_w_out[j].astype(BF16)
            rest = (g, wi, conv_dw_w[j], conv_dw_b[j].reshape(1, d), conv_ln_g[j].reshape(1, d),
                    conv_ln_b[j].reshape(1, d), wo)
            xc = _conv_layer(xc, mod[i], 0, 0, ctx_len, *rest)
            xs = _conv_layer(xs, mod[i], 1, 1, dec_len, *rest)
        else:
            wi = na_w_in[j].astype(BF16)
            wo = na_w_out[j].astype(BF16)
            last = i == DEPTH - 1
            xc, *ctx_kv = _na_ctx_layer(xc, mod[i], g, wi, wo, ctx_len, final_g=fg if last else None, prev=ctx_kv)
            q, k, v, sz = _na_proj(xs, mod[i], 1, dec_len, NA_TILE, g, wi)
            att = _na_att(q, k, v, ck, cv, bias_tables, j, n_dec)
            xs = _na_out(xs, att, sz, mod[i], 1, dec_len, NA_TILE, wo, final_g=fg if last else None)

    y_prompt = xc.reshape(n_ctx, ctx_len, d)
    y_sample = xs.reshape(n_dec, dec_len, d)
    new_cache_k, new_cache_v = (t.reshape(n_ctx, n_na, ctx_len, N_HEADS, HEAD_DIM) for t in ctx_kv)
    return (y_prompt, y_sample, new_cache_k, new_cache_v)
```

```python
import functools

import numpy as np
import jax
import jax.numpy as jnp
from jax import lax
from jax.experimental import pallas as pl
from jax.experimental.pallas import tpu as pltpu

F32 = jnp.float32
BF16 = jnp.bfloat16

D_MODEL = 1024
LANES = 128
N_COLB = D_MODEL // LANES
DEPTH = 4
N_HEADS = 16
HEAD_DIM = 64
GRID_W = 64
GRID_ROWS = 32
CONV_K = 31
CONV_HALO = 16
ROW_WIN = 8
COL_WIN = 16
EPS = 1e-6
NEG_INF = -1e30
LOG2_E = 1.4426950408889634
MOD_ROWS = 16

Q_ROWS = 4
Q_BLK = Q_ROWS * GRID_W
K_ROWS = 12
K_BLK = K_ROWS * GRID_W
N_QBLK = GRID_ROWS // Q_ROWS

VMEM_LIMIT_BYTES = 56 * 1024 * 1024


def _params(sem):
    return pltpu.CompilerParams(dimension_semantics=sem, vmem_limit_bytes=VMEM_LIMIT_BYTES)


def _sigmoid(x):
    return 1.0 / (1.0 + jnp.exp(-x))


def _silu(x):
    return x * _sigmoid(x)


def _rms(x):
    return x * lax.rsqrt(jnp.mean(x * x, axis=-1, keepdims=True) + EPS)


def _dot(a, b):
    return jnp.dot(a, b, preferred_element_type=F32)


def _dot_nt(a, b):
    return lax.dot_general(a, b, (((1,), (1,)), ((), ())), preferred_element_type=F32)


ADA_TILE = 768


def _ada_body(c_ref, w_ref, b_ref, o_ref):
    s = _silu(c_ref[...]).astype(BF16)
    o_ref[...] = _dot(s, w_ref[...].astype(BF16)) + b_ref[...]


def _ada(cvec, ada_w, ada_b):
    n_out = 3 * D_MODEL
    return pl.pallas_call(
        _ada_body,
        grid=(DEPTH, n_out // ADA_TILE),
        in_specs=[pl.BlockSpec((MOD_ROWS, D_MODEL), lambda l, n: (0, 0)),
                  pl.BlockSpec((None, D_MODEL, ADA_TILE), lambda l, n: (l, 0, n)),
                  pl.BlockSpec((None, 1, ADA_TILE), lambda l, n: (l, 0, n))],
        out_specs=pl.BlockSpec((None, MOD_ROWS, ADA_TILE), lambda l, n: (l, 0, n)),
        out_shape=jax.ShapeDtypeStruct((DEPTH, MOD_ROWS, n_out), F32),
        compiler_params=_params(("arbitrary", "arbitrary")),
        name="ada_params",
    )(cvec, ada_w, ada_b.reshape(DEPTH, 1, n_out))


def _modulate(x, gs, shift):
    return (_rms(x) * gs + shift).astype(BF16)


N_SEG = 8
CONV_TILE = 1024
SEG_LEN = CONV_TILE // N_SEG
PRE_ROWS = 64
ROW_CHUNK = 128
COL_W = 256
NORM_ROWS = 32
COL_CHUNK = 512


def _colblock_spec(rows):
    return pl.BlockSpec((N_COLB, rows, LANES), lambda i: (0, i, 0))


def _load_colblocks(x_ref):
    return jnp.concatenate([x_ref[cb] for cb in range(N_COLB)], axis=-1)


def _conv_body(*refs, tile, ext_halo, tiles_per_seq, segs_per_seq):
    x_refs, refs = refs[:N_COLB], refs[N_COLB:]
    if ext_halo:
        (xp_ref, xn_ref, mod_ref, g_ref, win_ref, dww_ref, dwb_ref, lng_ref, lnb_ref, wout_ref,
         o_ref, h_s, u_s, sz_s, gt_s, v_s, wb_s, xp_s) = refs
    else:
        (mod_ref, g_ref, win_ref, dww_ref, dwb_ref, lng_ref, lnb_ref, wout_ref,
         o_ref, h_s, u_s, sz_s, gt_s, v_s, wb_s, xp_s) = refs
    d = D_MODEL
    seg = tile // N_SEG
    halo_rows = CONV_HALO * N_SEG
    main0 = halo_rows
    back0 = main0 + tile
    shift = mod_ref[0:1, :]
    gate = mod_ref[2:3, :]
    gs = g_ref[...] * (1.0 + mod_ref[1:2, :])

    def seg_rows(a):
        return pl.ds(a, N_SEG, stride=seg)

    def load_x(a):
        return jnp.concatenate([xr[seg_rows(a), :] for xr in x_refs], axis=-1)

    def permute(a, carry):
        xp_s[pl.ds(pl.multiple_of(a * N_SEG, N_SEG), N_SEG), :] = load_x(a)
        return carry

    def modulate_rows(c, carry):
        rows = pl.ds(pl.multiple_of(c * PRE_ROWS, PRE_ROWS), PRE_ROWS)
        h_s[rows, :] = _modulate(xp_s[rows, :], gs, shift)
        return carry

    lax.fori_loop(0, seg, permute, 0, unroll=16)
    lax.fori_loop(0, tile // PRE_ROWS, modulate_rows, 0, unroll=2)
    n_rc = tile // ROW_CHUNK
    n_cc = d // COL_W
    for k in range(CONV_K):
        wb_s[k * N_SEG:(k + 1) * N_SEG, :] = jnp.broadcast_to(dww_ref[k:k + 1, :], (N_SEG, d))
    wb_s[CONV_K * N_SEG:(CONV_K + 1) * N_SEG, :] = jnp.broadcast_to(dwb_ref[...], (N_SEG, d))

    def row0(r):
        return r * ROW_CHUNK if isinstance(r, int) else pl.multiple_of(r * ROW_CHUNK, ROW_CHUNK)

    sub = lax.broadcasted_iota(jnp.int32, (N_SEG, 1), 0)
    if ext_halo:
        i = pl.program_id(0)
        h_s[tile:tile + CONV_HALO, :] = _modulate(xp_ref[...], gs, shift)
        h_s[tile + CONV_HALO:tile + 2 * CONV_HALO, :] = _modulate(xn_ref[...], gs, shift)
        he = h_s[tile:tile + 2 * CONV_HALO, :]
        for cb in range(d // COL_CHUNK):
            cs = cb * COL_CHUNK
            a = _dot(he, win_ref[:, cs:cs + COL_CHUNK])
            b = _dot(he, win_ref[:, d + cs:d + cs + COL_CHUNK])
            u_s[0:2 * CONV_HALO, cs:cs + COL_CHUNK] = a * _sigmoid(b)
        starts_seq = sub == 0
        ends_seq = sub == N_SEG - 1
    else:
        starts_seq = sub % segs_per_seq == 0
        ends_seq = sub % segs_per_seq == segs_per_seq - 1

    def fill_halo(j):
        cols = slice(j * COL_W, (j + 1) * COL_W)
        if ext_halo:
            u_prev = jnp.where(i % tiles_per_seq == 0, 0.0, u_s[0:CONV_HALO, cols])
            u_next = jnp.where(i % tiles_per_seq == tiles_per_seq - 1, 0.0, u_s[CONV_HALO:2 * CONV_HALO, cols])
        front = u_s[back0 - halo_rows - 1:back0 - 1, cols]
        back = u_s[main0 + 1:main0 + 1 + halo_rows, cols]
        for p in range(CONV_HALO):
            rows = slice(p * N_SEG, (p + 1) * N_SEG)
            edge_f = jnp.broadcast_to(u_prev[p:p + 1, :], (N_SEG, COL_W)) if ext_halo else 0.0
            edge_b = jnp.broadcast_to(u_next[p:p + 1, :], (N_SEG, COL_W)) if ext_halo else 0.0
            u_s[p * N_SEG:(p + 1) * N_SEG, cols] = jnp.where(starts_seq, edge_f, front[rows])
            u_s[back0 + p * N_SEG:back0 + (p + 1) * N_SEG, cols] = jnp.where(ends_seq, edge_b, back[rows])

    def conv_chunk(j, r):
        r0 = row0(r)
        for cb in range(j * COL_W // LANES, (j + 1) * COL_W // LANES):
            lanes = slice(cb * LANES, (cb + 1) * LANES)
            n_pos = ROW_CHUNK // N_SEG
            accs = [wb_s[CONV_K * N_SEG:(CONV_K + 1) * N_SEG, lanes]] * n_pos
            for k in range(CONV_K):
                wk = wb_s[k * N_SEG:(k + 1) * N_SEG, lanes]
                k0 = r0 + (k + CONV_HALO - CONV_K // 2) * N_SEG
                accs = [acc + u_s[pl.ds(k0 + p * N_SEG, N_SEG), lanes] * wk for p, acc in enumerate(accs)]
            v_s[pl.ds(r0, ROW_CHUNK), lanes] = jnp.concatenate(accs, axis=0)

    for cb in range(d // COL_CHUNK):
        cs = cb * COL_CHUNK
        hm = h_s[0:tile, :]
        a = _dot(hm, win_ref[:, cs:cs + COL_CHUNK])
        b = _dot(hm, win_ref[:, d + cs:d + cs + COL_CHUNK])
        u_s[main0:main0 + tile, cs:cs + COL_CHUNK] = a * _sigmoid(b)
        z = _dot(hm, win_ref[:, 2 * d + cs:2 * d + cs + COL_CHUNK])
        sz_s[:, cs:cs + COL_CHUNK] = _silu(z)
    for j in range(n_cc):
        fill_halo(j)

    def conv_rows(r, carry):
        for j in range(n_cc):
            conv_chunk(j, r)
        return carry

    def norm_gate(r):
        for part in range(ROW_CHUNK // NORM_ROWS):
            rows = pl.ds(row0(r) + part * NORM_ROWS, NORM_ROWS)
            acc = v_s[rows, :]
            mu = jnp.mean(acc, axis=-1, keepdims=True)
            dev = acc - mu
            var = jnp.mean(dev * dev, axis=-1, keepdims=True)
            y = dev * lax.rsqrt(var + EPS) * lng_ref[...] + lnb_ref[...]
            gt_s[rows, :] = (_silu(y) * sz_s[rows, :]).astype(BF16)

    def norm_rows(r, carry):
        norm_gate(r)
        return carry

    lax.fori_loop(0, n_rc, conv_rows, 0)
    lax.fori_loop(0, n_rc, norm_rows, 0)
    xp_s[...] = xp_s[...] + gate * _dot(gt_s[...], wout_ref[...])

    def post(a, carry):
        y = xp_s[pl.ds(pl.multiple_of(a * N_SEG, N_SEG), N_SEG), :]
        for cb in range(N_COLB):
            o_ref[cb, seg_rows(a), :] = y[:, cb * LANES:(cb + 1) * LANES]
        return carry

    lax.fori_loop(0, seg, post, 0, unroll=16)


def _mod_spec(row0, row_step, tiles_per_seq):
    return pl.BlockSpec((None, 3, D_MODEL), lambda i: (row0 + row_step * (i // tiles_per_seq), 0, 0))


def _resident(shape):
    return pl.BlockSpec(shape, lambda *_: (0,) * len(shape), pipeline_mode=pl.Buffered(1))


def _conv_layer(x, mod, row0, row_step, seq_len, g, w_in, dw_w, dw_b, ln_g, ln_b, w_out):
    n_tok, d = x.shape
    tile = CONV_TILE
    seg = SEG_LEN
    segs_per_seq = seq_len // seg
    ext_halo = segs_per_seq > N_SEG
    tiles_per_seq = max(segs_per_seq // N_SEG, 1)
    assert (segs_per_seq % N_SEG == 0) if ext_halo else (N_SEG % segs_per_seq == 0)
    hb = tile // CONV_HALO
    n_hblk = n_tok // CONV_HALO
    in_specs = [pl.BlockSpec((tile, LANES), lambda i, cb=cb: (i, cb)) for cb in range(N_COLB)]
    args = [x] * N_COLB
    if ext_halo:
        in_specs += [pl.BlockSpec((CONV_HALO, d), lambda i: (jnp.maximum(i * hb - 1, 0), 0)),
                     pl.BlockSpec((CONV_HALO, d), lambda i: (jnp.minimum((i + 1) * hb, n_hblk - 1), 0))]
        args += [x, x]
    in_specs += [_mod_spec(row0, row_step, tiles_per_seq),
                 _resident((1, d)), _resident((d, 3 * d)), _resident((CONV_K, d)),
                 _resident((1, d)), _resident((1, d)), _resident((1, d)), _resident((d, d))]
    args += [mod, g, w_in, dw_w, dw_b, ln_g, ln_b, w_out]
    h_rows = tile + 2 * CONV_HALO if ext_halo else tile
    return pl.pallas_call(
        functools.partial(_conv_body, tile=tile, ext_halo=ext_halo, tiles_per_seq=tiles_per_seq,
                          segs_per_seq=segs_per_seq),
        grid=(n_tok // tile,),
        in_specs=in_specs,
        out_specs=_colblock_spec(tile),
        out_shape=jax.ShapeDtypeStruct((N_COLB, n_tok, LANES), F32),
        scratch_shapes=[pltpu.VMEM((h_rows, d), BF16),
                        pltpu.VMEM((tile + 2 * CONV_HALO * N_SEG, d), F32),
                        pltpu.VMEM((tile, d), F32),
                        pltpu.VMEM((tile, d), BF16),
                        pltpu.VMEM((tile, d), F32),
                        pltpu.VMEM(((CONV_K + 1) * N_SEG, d), F32),
                        pltpu.VMEM((tile, d), F32)],
        compiler_params=_params(("arbitrary",)),
        name="conv_layer",
    )(*args)


def _head_masks():
    lane = lax.broadcasted_iota(jnp.int32, (1, 2 * HEAD_DIM), 1)
    first = lane < HEAD_DIM
    return first, jnp.logical_not(first)


def _na_ctx_body(*refs, final, n_prev):
    x_ref, mod_ref, g_ref, win_ref, wout_ref = refs[:5]
    fg_ref = refs[5] if final else None
    o_ref, k_ref, v_ref, q_s, kb_s, vb_s, sz_s, att_s = refs[-8:]
    d = D_MODEL
    if n_prev:
        kprev_ref, vprev_ref = refs[-10:-8]
        k_ref[0:n_prev] = kprev_ref[...]
        v_ref[0:n_prev] = vprev_ref[...]
    x = _load_colblocks(x_ref)
    gs = g_ref[...] * (1.0 + mod_ref[1:2, :])
    h = _modulate(x, gs, mod_ref[0:1, :])
    for cb in range(d // COL_CHUNK):
        cs = cb * COL_CHUNK
        q_s[:, cs:cs + COL_CHUNK] = (_dot(h, win_ref[:, cs:cs + COL_CHUNK]) * (HEAD_DIM ** -0.5)).astype(BF16)
        k = _dot(h, win_ref[:, d + cs:d + cs + COL_CHUNK])
        k_ref[n_prev, :, cs:cs + COL_CHUNK] = k
        kb_s[:, cs:cs + COL_CHUNK] = k.astype(BF16)
        v = _dot(h, win_ref[:, 2 * d + cs:2 * d + cs + COL_CHUNK])
        v_ref[n_prev, :, cs:cs + COL_CHUNK] = v
        vb_s[:, cs:cs + COL_CHUNK] = v.astype(BF16)
        sz_s[:, cs:cs + COL_CHUNK] = _silu(_dot(h, win_ref[:, 3 * d + cs:3 * d + cs + COL_CHUNK]))

    masks = _head_masks()
    for hp in range(N_HEADS // 2):
        ls = hp * 2 * HEAD_DIM
        q = q_s[:, ls:ls + 2 * HEAD_DIM]
        kp = kb_s[:, ls:ls + 2 * HEAD_DIM]
        vp = vb_s[:, ls:ls + 2 * HEAD_DIM]
        outs = []
        for hh in range(2):
            s = _dot_nt(jnp.where(masks[hh], q, jnp.zeros_like(q)), kp)
            p = jnp.exp(s - jnp.max(s, axis=-1, keepdims=True))
            l = jnp.sum(p, axis=-1, keepdims=True)
            outs.append(_dot(p.astype(BF16), vp) / l)
        att = jnp.where(masks[0], outs[0], outs[1])
        att_s[:, ls:ls + 2 * HEAD_DIM] = (att * sz_s[:, ls:ls + 2 * HEAD_DIM]).astype(BF16)

    y = x + mod_ref[2:3, :] * _dot(att_s[...], wout_ref[...])
    if final:
        y = _rms(y) * fg_ref[...]
    o_ref[...] = y


def _na_ctx_layer(x, mod, g, w_in, w_out, seq_len, final_g=None, prev=None):
    n_tok, d = x.shape[1], D_MODEL
    n_seq = n_tok // seq_len
    n_prev = 0 if prev is None else prev[0].shape[1]
    const = lambda b: (0, 0)
    final = final_g is not None
    in_specs = [_colblock_spec(seq_len),
                pl.BlockSpec((None, 3, d), lambda b: (0, 0, 0)),
                pl.BlockSpec((1, d), const),
                pl.BlockSpec((d, 4 * d), const),
                pl.BlockSpec((d, d), const)]
    args = [x, mod, g, w_in, w_out]
    if final:
        in_specs.append(pl.BlockSpec((1, d), const))
        args.append(final_g)
    if n_prev:
        in_specs += [pl.BlockSpec((None, n_prev, seq_len, d), lambda b: (b, 0, 0, 0))] * 2
        args += list(prev)
    stack_spec = pl.BlockSpec((None, n_prev + 1, seq_len, d), lambda b: (b, 0, 0, 0))
    stack_shape = jax.ShapeDtypeStruct((n_seq, n_prev + 1, seq_len, d), F32)
    return pl.pallas_call(
        functools.partial(_na_ctx_body, final=final, n_prev=n_prev),
        grid=(n_seq,),
        in_specs=in_specs,
        out_specs=[pl.BlockSpec((seq_len, d), lambda b: (b, 0)), stack_spec, stack_spec],
        out_shape=[jax.ShapeDtypeStruct((n_tok, d), F32), stack_shape, stack_shape],
        scratch_shapes=[pltpu.VMEM((seq_len, d), BF16),
                        pltpu.VMEM((seq_len, d), BF16),
                        pltpu.VMEM((seq_len, d), BF16),
                        pltpu.VMEM((seq_len, d), F32),
                        pltpu.VMEM((seq_len, d), BF16)],
        compiler_params=_params(("arbitrary",)),
        name="na_ctx_layer",
    )(*args)


def _na_proj_body(x_ref, mod_ref, g_ref, win_ref, q_ref, k_ref, v_ref, sz_ref):
    d = D_MODEL
    gs = g_ref[...] * (1.0 + mod_ref[1:2, :])
    h = _modulate(_load_colblocks(x_ref), gs, mod_ref[0:1, :])
    for cb in range(d // COL_CHUNK):
        cs = cb * COL_CHUNK
        q_ref[:, cs:cs + COL_CHUNK] = (_dot(h, win_ref[:, cs:cs + COL_CHUNK]) * (HEAD_DIM ** -0.5 * LOG2_E)).astype(BF16)
        k_ref[:, cs:cs + COL_CHUNK] = _dot(h, win_ref[:, d + cs:d + cs + COL_CHUNK]).astype(BF16)
        v_ref[:, cs:cs + COL_CHUNK] = _dot(h, win_ref[:, 2 * d + cs:2 * d + cs + COL_CHUNK]).astype(BF16)
        sz_ref[:, cs:cs + COL_CHUNK] = _silu(_dot(h, win_ref[:, 3 * d + cs:3 * d + cs + COL_CHUNK])).astype(BF16)


def _na_proj(x, mod, row0, seq_len, tile, g, w_in):
    n_tok, d = x.shape[1], D_MODEL
    tiles_per_seq = seq_len // tile
    const = lambda i: (0, 0)
    tok_spec = pl.BlockSpec((tile, d), lambda i: (i, 0))
    out = jax.ShapeDtypeStruct((n_tok, d), BF16)
    return pl.pallas_call(
        _na_proj_body,
        grid=(n_tok // tile,),
        in_specs=[_colblock_spec(tile),
                  _mod_spec(row0, 1, tiles_per_seq),
                  pl.BlockSpec((1, d), const),
                  pl.BlockSpec((d, 4 * d), const)],
        out_specs=[tok_spec] * 4,
        out_shape=[out] * 4,
        compiler_params=_params(("arbitrary",)),
        name="na_proj",
    )(x, mod, g, w_in)


def _block_rows(qb):
    ks = min(max(Q_ROWS * qb - ROW_WIN // 2, 0), GRID_ROWS - K_ROWS)
    rs = [min(max(Q_ROWS * qb + qr - ROW_WIN // 2, 0), GRID_ROWS - ROW_WIN) for qr in range(Q_ROWS)]
    return ks, rs


_CLASS_BLOCKS = (0, 2, N_QBLK - 1)


def _na_att_body(q_ref, k_ref, v_ref, kc_ref, vc_ref, t2_ref, o_ref, bias_s, kc_s, vc_s):
    masks = _head_masks()
    first = masks[0]
    kc_s[...] = kc_ref[...].astype(BF16)

    def with_ones(v, hh):
        return jnp.where(masks[hh], v, jnp.ones_like(v))

    vc = vc_ref[...].astype(BF16)
    for hh in range(2):
        vc_s[hh] = with_ones(vc, hh)

    def tile_rows(qb, qr, m):
        ks, rs = _block_rows(qb)
        ka = ks + 2 * m
        va = rs[qr] <= ka < rs[qr] + ROW_WIN
        vb = rs[qr] <= ka + 1 < rs[qr] + ROW_WIN
        return va, vb, ka - (Q_ROWS * qb + qr) + ROW_WIN

    @pl.when(pl.program_id(1) == 0)
    def _():
        neg = jnp.full((GRID_W, 2 * GRID_W), NEG_INF, F32)
        for cls, qb in enumerate(_CLASS_BLOCKS):
            for hh in range(2):
                for qr in range(Q_ROWS):
                    for m in range(K_ROWS // 2):
                        va, vb, e = tile_rows(qb, qr, m)
                        if not (va or vb):
                            continue
                        t = t2_ref[hh, e]
                        if not va:
                            t = jnp.where(first, neg, t)
                        if not vb:
                            t = jnp.where(first, t, neg)
                        bias_s[cls * 2 + hh, qr * GRID_W:(qr + 1) * GRID_W,
                               m * 2 * GRID_W:(m + 1) * 2 * GRID_W] = t

    def block(q0, k0, cls):
        q = q_ref[pl.ds(q0, Q_BLK), :]
        kl = k_ref[pl.ds(k0, K_BLK), :]
        vl = v_ref[pl.ds(k0, K_BLK), :]
        qb = _CLASS_BLOCKS[cls]
        outs = []
        for hh in range(2):
            qm = jnp.where(masks[hh], q, jnp.zeros_like(q))
            sl = _dot_nt(qm, kl)
            sc = _dot_nt(qm, kc_s[...])
            p_l, p_c = [], []
            for qr in range(Q_ROWS):
                rows = slice(qr * GRID_W, (qr + 1) * GRID_W)
                tiles = {}
                for m in range(K_ROWS // 2):
                    va, vb, _ = tile_rows(qb, qr, m)
                    if va or vb:
                        cols = slice(m * 2 * GRID_W, (m + 1) * 2 * GRID_W)
                        tiles[m] = sl[rows, cols] + bias_s[cls * 2 + hh, rows, cols]
                s_ctx = sc[rows, :]
                mx = jnp.max(s_ctx, axis=-1, keepdims=True)
                for t in tiles.values():
                    mx = jnp.maximum(mx, jnp.max(t, axis=-1, keepdims=True))
                zero = jnp.zeros((GRID_W, 2 * GRID_W), BF16)
                p_l.append(jnp.concatenate(
                    [jnp.exp2(tiles[m] - mx).astype(BF16) if m in tiles else zero for m in range(K_ROWS // 2)],
                    axis=1))
                p_c.append(jnp.exp2(s_ctx - mx).astype(BF16))
            o = (_dot(jnp.concatenate(p_l, axis=0), with_ones(vl, hh))
                 + _dot(jnp.concatenate(p_c, axis=0), vc_s[hh]))
            outs.append(o / pltpu.roll(o, HEAD_DIM, 1))
        o_ref[pl.ds(q0, Q_BLK), :] = jnp.where(first, outs[0], outs[1]).astype(BF16)

    block(0, _block_rows(0)[0] * GRID_W, 0)
    block((N_QBLK - 1) * Q_BLK, _block_rows(N_QBLK - 1)[0] * GRID_W, 2)

    def interior(qb, carry):
        q0 = pl.multiple_of(qb * Q_BLK, Q_BLK)
        block(q0, pl.multiple_of(q0 - (ROW_WIN // 2) * GRID_W, Q_BLK), 1)
        return carry

    lax.fori_loop(1, N_QBLK - 1, interior, 0, unroll=3)


def _na_att(q, k, v, cache_k, cache_v, t2, layer_j, n_batch):
    n_tok, d = q.shape
    seq_len = n_tok // n_batch
    past = cache_k.shape[2]
    pw = 2 * HEAD_DIM
    tok_spec = pl.BlockSpec((seq_len, pw), lambda hp, b: (b, hp))
    cache_spec = pl.BlockSpec((None, None, past, pw), lambda hp, b: (b, layer_j, 0, hp))
    return pl.pallas_call(
        _na_att_body,
        grid=(N_HEADS // 2, n_batch),
        in_specs=[tok_spec, tok_spec, tok_spec, cache_spec, cache_spec,
                  pl.BlockSpec((None, 2, 2 * ROW_WIN, GRID_W, pw), lambda hp, b: (layer_j, hp, 0, 0, 0))],
        out_specs=tok_spec,
        out_shape=jax.ShapeDtypeStruct((n_tok, d), BF16),
        scratch_shapes=[pltpu.VMEM((3 * 2, Q_BLK, K_BLK), F32),
                        pltpu.VMEM((past, pw), BF16),
                        pltpu.VMEM((2, past, pw), BF16)],
        compiler_params=_params(("arbitrary", "arbitrary")),
        name="na_attention",
    )(q, k, v, cache_k, cache_v, t2)


def _na_out_body(*refs, final):
    if final:
        x_ref, att_ref, sz_ref, mod_ref, wout_ref, fg_ref, o_ref = refs
    else:
        x_ref, att_ref, sz_ref, mod_ref, wout_ref, o_ref = refs
    a = (att_ref[...].astype(F32) * sz_ref[...].astype(F32)).astype(BF16)
    y = _load_colblocks(x_ref) + mod_ref[2:3, :] * _dot(a, wout_ref[...])
    if final:
        y = _rms(y) * fg_ref[...]
    o_ref[...] = y


def _na_out(x, att, sz, mod, row0, seq_len, tile, w_out, final_g=None):
    n_tok, d = x.shape[1], D_MODEL
    tiles_per_seq = seq_len // tile
    const = lambda i: (0, 0)
    tok_spec = pl.BlockSpec((tile, d), lambda i: (i, 0))
    final = final_g is not None
    in_specs = [_colblock_spec(tile), tok_spec, tok_spec,
                _mod_spec(row0, 1, tiles_per_seq),
                pl.BlockSpec((d, d), const)]
    args = [x, att, sz, mod, w_out]
    if final:
        in_specs.append(pl.BlockSpec((1, d), const))
        args.append(final_g)
    return pl.pallas_call(
        functools.partial(_na_out_body, final=final),
        grid=(n_tok // tile,),
        in_specs=in_specs,
        out_specs=tok_spec,
        out_shape=jax.ShapeDtypeStruct((n_tok, d), F32),
        compiler_params=_params(("arbitrary",)),
        name="na_out",
    )(*args)


def _bias_tables(rpb):
    n_dr, n_dc = 2 * ROW_WIN - 1, 2 * COL_WIN - 1
    qc = np.arange(GRID_W)[:, None]
    lane = np.arange(2 * GRID_W)[None, :]
    kc, half = lane % GRID_W, lane // GRID_W
    start = np.clip(qc - COL_WIN // 2, 0, GRID_W - COL_WIN)
    in_win = (kc >= start) & (kc < start + COL_WIN)
    dc = kc - qc + COL_WIN - 1
    onehot = ((np.arange(n_dc)[:, None, None] == dc[None]) & in_win[None])[None] \
        & (np.arange(2)[:, None, None, None] == half[None, None])
    padded = jnp.pad(rpb, ((0, 0), (0, 0), (1, 1), (0, 0)))
    rows = jnp.stack([padded[:, :, :n_dr + 1], padded[:, :, 1:n_dr + 2]], axis=3)
    t = jnp.einsum("lhexj,xjqk->lheqk", rows, jnp.asarray(onehot, F32), precision=lax.Precision.HIGHEST)
    dr = np.arange(n_dr + 1)[:, None, None] - 1 + half[None]
    valid = (dr >= 0) & (dr < n_dr) & in_win[None]
    return jnp.where(jnp.asarray(valid), t * LOG2_E, NEG_INF)


NA_TILE = 1024


def kernel(x_prompt, x_sample, c, cache_k, cache_v, c_ctx, norm_g, ada_w, ada_b, conv_w_in, conv_dw_w,
           conv_dw_b, conv_ln_g, conv_ln_b, conv_w_out, na_w_in, na_rpb, na_w_out, final_g):
    n_ctx, ctx_len, d = x_prompt.shape
    n_dec, dec_len, _ = x_sample.shape
    n_na = na_w_in.shape[0]
    past = cache_k.shape[2]

    cvec = jnp.concatenate([c_ctx[None], c, jnp.zeros((MOD_ROWS - 1 - n_dec, d), F32)], axis=0)
    mod = _ada(cvec, ada_w, ada_b).reshape(DEPTH, MOD_ROWS, 3, d)

    xc = x_prompt.reshape(n_ctx * ctx_len, d)
    xs = x_sample.reshape(n_dec * dec_len, d)
    ck = cache_k.reshape(n_dec, n_na, past, d)
    cv = cache_v.reshape(n_dec, n_na, past, d)
    fg = final_g.reshape(1, d)
    bias_tables = _bias_tables(na_rpb)
    ctx_kv = None
    for i in range(DEPTH):
        j = i // 2
        g = norm_g[i].reshape(1, d)
        if i % 2 == 0:
            wi = conv_w_in[j].astype(BF16)
            wo = conv_w_out[j].astype(BF16)
            rest = (g, wi, conv_dw_w[j], conv_dw_b[j].reshape(1, d), conv_ln_g[j].reshape(1, d),
                    conv_ln_b[j].reshape(1, d), wo)
            xc = _conv_layer(xc, mod[i], 0, 0, ctx_len, *rest)
            xs = _conv_layer(xs, mod[i], 1, 1, dec_len, *rest)
        else:
            wi = na_w_in[j].astype(BF16)
            wo = na_w_out[j].astype(BF16)
            last = i == DEPTH - 1
            xc, *ctx_kv = _na_ctx_layer(xc, mod[i], g, wi, wo, ctx_len, final_g=fg if last else None, prev=ctx_kv)
            q, k, v, sz = _na_proj(xs, mod[i], 1, dec_len, NA_TILE, g, wi)
            att = _na_att(q, k, v, ck, cv, bias_tables, j, n_dec)
            xs = _na_out(xs, att, sz, mod[i], 1, dec_len, NA_TILE, wo, final_g=fg if last else None)

    y_prompt = xc.reshape(n_ctx, ctx_len, d)
    y_sample = xs.reshape(n_dec, dec_len, d)
    new_cache_k, new_cache_v = (t.reshape(n_ctx, n_na, ctx_len, N_HEADS, HEAD_DIM) for t in ctx_kv)
    return (y_prompt, y_sample, new_cache_k, new_cache_v)
```

```python
import functools

import numpy as np
import jax
import jax.numpy as jnp
from jax import lax
from jax.experimental import pallas as pl
from jax.experimental.pallas import tpu as pltpu

F32 = jnp.float32
BF16 = jnp.bfloat16

D_MODEL = 1024
LANES = 128
DEPTH = 4
N_HEADS = 16
HEAD_DIM = 64
GRID_W = 64
GRID_ROWS = 32
CONV_K = 31
CONV_HALO = 16
ROW_WIN = 8
COL_WIN = 16
EPS = 1e-6
NEG_INF = -1e30
LOG2_E = 1.4426950408889634
MOD_ROWS = 16

Q_ROWS = 4
Q_BLK = Q_ROWS * GRID_W
K_ROWS = 12
K_BLK = K_ROWS * GRID_W
N_QBLK = GRID_ROWS // Q_ROWS

VMEM_LIMIT_BYTES = 56 * 1024 * 1024


def _params(sem):
    return pltpu.CompilerParams(dimension_semantics=sem, vmem_limit_bytes=VMEM_LIMIT_BYTES)


def _sigmoid(x):
    return 1.0 / (1.0 + jnp.exp(-x))


def _silu(x):
    return x * _sigmoid(x)


def _rms(x):
    return x * lax.rsqrt(jnp.mean(x * x, axis=-1, keepdims=True) + EPS)


def _dot(a, b):
    return jnp.dot(a, b, preferred_element_type=F32)


def _dot_nt(a, b):
    return lax.dot_general(a, b, (((1,), (1,)), ((), ())), preferred_element_type=F32)


ADA_TILE = 768


def _ada_body(c_ref, w_ref, b_ref, o_ref):
    s = _silu(c_ref[...]).astype(BF16)
    o_ref[...] = _dot(s, w_ref[...].astype(BF16)) + b_ref[...]


def _ada(cvec, ada_w, ada_b):
    n_out = 3 * D_MODEL
    return pl.pallas_call(
        _ada_body,
        grid=(DEPTH, n_out // ADA_TILE),
        in_specs=[pl.BlockSpec((MOD_ROWS, D_MODEL), lambda l, n: (0, 0)),
                  pl.BlockSpec((None, D_MODEL, ADA_TILE), lambda l, n: (l, 0, n)),
                  pl.BlockSpec((None, 1, ADA_TILE), lambda l, n: (l, 0, n))],
        out_specs=pl.BlockSpec((None, MOD_ROWS, ADA_TILE), lambda l, n: (l, 0, n)),
        out_shape=jax.ShapeDtypeStruct((DEPTH, MOD_ROWS, n_out), F32),
        compiler_params=_params(("arbitrary", "arbitrary")),
        name="ada_params",
    )(cvec, ada_w, ada_b.reshape(DEPTH, 1, n_out))


def _modulate(x, gs, shift):
    return (_rms(x) * gs + shift).astype(BF16)


N_SEG = 8
CONV_TILE = 1024
SEG_LEN = CONV_TILE // N_SEG
PRE_ROWS = 64
ROW_CHUNK = 128
COL_W = 256
NORM_ROWS = 32
COL_CHUNK = 512


def _segment_permutation(tile):
    seg = tile // N_SEG
    p = np.zeros((tile, tile), np.float32)
    dst = np.arange(tile)
    p[dst, (dst % N_SEG) * seg + dst // N_SEG] = 1.0
    return p


def _conv_body(*refs, tile, ext_halo, tiles_per_seq, segs_per_seq):
    x_ref, refs = refs[0], refs[1:]
    if ext_halo:
        (xp_ref, xn_ref, mod_ref, g_ref, win_ref, dww_ref, dwb_ref, lng_ref, lnb_ref, wout_ref, perm_ref, unperm_ref,
         o_ref, h_s, u_s, sz_s, gt_s, v_s, wb_s, hn_s) = refs
    else:
        (mod_ref, g_ref, win_ref, dww_ref, dwb_ref, lng_ref, lnb_ref, wout_ref, perm_ref, unperm_ref,
         o_ref, h_s, u_s, sz_s, gt_s, v_s, wb_s, hn_s) = refs
    d = D_MODEL
    halo_rows = CONV_HALO * N_SEG
    main0 = halo_rows
    back0 = main0 + tile
    shift = mod_ref[0:1, :]
    gate = mod_ref[2:3, :]
    gs = g_ref[...] * (1.0 + mod_ref[1:2, :])

    def modulate_rows(c, carry):
        rows = pl.ds(pl.multiple_of(c * PRE_ROWS, PRE_ROWS), PRE_ROWS)
        hn_s[rows, :] = _modulate(x_ref[rows, :], gs, shift)
        return carry

    lax.fori_loop(0, tile // PRE_ROWS, modulate_rows, 0, unroll=2)
    for cb in range(d // COL_CHUNK):
        cols = slice(cb * COL_CHUNK, (cb + 1) * COL_CHUNK)
        h_s[0:tile, cols] = _dot(perm_ref[...], hn_s[:, cols]).astype(BF16)
    n_rc = tile // ROW_CHUNK
    n_cc = d // COL_W
    for k in range(CONV_K):
        wb_s[k * N_SEG:(k + 1) * N_SEG, :] = jnp.broadcast_to(dww_ref[k:k + 1, :], (N_SEG, d))
    wb_s[CONV_K * N_SEG:(CONV_K + 1) * N_SEG, :] = jnp.broadcast_to(dwb_ref[...], (N_SEG, d))

    def row0(r):
        return r * ROW_CHUNK if isinstance(r, int) else pl.multiple_of(r * ROW_CHUNK, ROW_CHUNK)

    sub = lax.broadcasted_iota(jnp.int32, (N_SEG, 1), 0)
    if ext_halo:
        i = pl.program_id(0)
        h_s[tile:tile + CONV_HALO, :] = _modulate(xp_ref[...], gs, shift)
        h_s[tile + CONV_HALO:tile + 2 * CONV_HALO, :] = _modulate(xn_ref[...], gs, shift)
        he = h_s[tile:tile + 2 * CONV_HALO, :]
        for cb in range(d // COL_CHUNK):
            cs = cb * COL_CHUNK
            a = _dot(he, win_ref[:, cs:cs + COL_CHUNK])
            b = _dot(he, win_ref[:, d + cs:d + cs + COL_CHUNK])
            u_s[0:2 * CONV_HALO, cs:cs + COL_CHUNK] = a * _sigmoid(b)
        starts_seq = sub == 0
        ends_seq = sub == N_SEG - 1
    else:
        starts_seq = sub % segs_per_seq == 0
        ends_seq = sub % segs_per_seq == segs_per_seq - 1

    def fill_halo(j):
        cols = slice(j * COL_W, (j + 1) * COL_W)
        if ext_halo:
            u_prev = jnp.where(i % tiles_per_seq == 0, 0.0, u_s[0:CONV_HALO, cols])
            u_next = jnp.where(i % tiles_per_seq == tiles_per_seq - 1, 0.0, u_s[CONV_HALO:2 * CONV_HALO, cols])
        front = u_s[back0 - halo_rows - 1:back0 - 1, cols]
        back = u_s[main0 + 1:main0 + 1 + halo_rows, cols]
        for p in range(CONV_HALO):
            rows = slice(p * N_SEG, (p + 1) * N_SEG)
            edge_f = jnp.broadcast_to(u_prev[p:p + 1, :], (N_SEG, COL_W)) if ext_halo else 0.0
            edge_b = jnp.broadcast_to(u_next[p:p + 1, :], (N_SEG, COL_W)) if ext_halo else 0.0
            u_s[p * N_SEG:(p + 1) * N_SEG, cols] = jnp.where(starts_seq, edge_f, front[rows])
            u_s[back0 + p * N_SEG:back0 + (p + 1) * N_SEG, cols] = jnp.where(ends_seq, edge_b, back[rows])

    def conv_chunk(j, r):
        r0 = row0(r)
        for cb in range(j * COL_W // LANES, (j + 1) * COL_W // LANES):
            lanes = slice(cb * LANES, (cb + 1) * LANES)
            n_pos = ROW_CHUNK // N_SEG
            accs = [wb_s[CONV_K * N_SEG:(CONV_K + 1) * N_SEG, lanes]] * n_pos
            for k in range(CONV_K):
                wk = wb_s[k * N_SEG:(k + 1) * N_SEG, lanes]
                k0 = r0 + (k + CONV_HALO - CONV_K // 2) * N_SEG
                accs = [acc + u_s[pl.ds(k0 + p * N_SEG, N_SEG), lanes] * wk for p, acc in enumerate(accs)]
            v_s[pl.ds(r0, ROW_CHUNK), lanes] = jnp.concatenate(accs, axis=0)

    for cb in range(d // COL_CHUNK):
        cs = cb * COL_CHUNK
        hm = h_s[0:tile, :]
        a = _dot(hm, win_ref[:, cs:cs + COL_CHUNK])
        b = _dot(hm, win_ref[:, d + cs:d + cs + COL_CHUNK])
        u_s[main0:main0 + tile, cs:cs + COL_CHUNK] = a * _sigmoid(b)
        z = _dot(hm, win_ref[:, 2 * d + cs:2 * d + cs + COL_CHUNK])
        sz_s[:, cs:cs + COL_CHUNK] = _silu(z)
    for j in range(n_cc):
        fill_halo(j)

    def conv_rows(r, carry):
        for j in range(n_cc):
            conv_chunk(j, r)
        return carry

    def norm_gate(r):
        for part in range(ROW_CHUNK // NORM_ROWS):
            rows = pl.ds(row0(r) + part * NORM_ROWS, NORM_ROWS)
            acc = v_s[rows, :]
            mu = jnp.mean(acc, axis=-1, keepdims=True)
            dev = acc - mu
            var = jnp.mean(dev * dev, axis=-1, keepdims=True)
            y = dev * lax.rsqrt(var + EPS) * lng_ref[...] + lnb_ref[...]
            gt_s[rows, :] = (_silu(y) * sz_s[rows, :]).astype(BF16)

    def norm_rows(r, carry):
        norm_gate(r)
        return carry

    lax.fori_loop(0, n_rc, conv_rows, 0)
    lax.fori_loop(0, n_rc, norm_rows, 0)
    for cb in range(d // COL_CHUNK):
        cols = slice(cb * COL_CHUNK, (cb + 1) * COL_CHUNK)
        hn_s[:, cols] = _dot(unperm_ref[...], gt_s[:, cols]).astype(BF16)
    o_ref[...] = x_ref[...] + gate * _dot(hn_s[...], wout_ref[...])


def _mod_spec(row0, row_step, tiles_per_seq):
    return pl.BlockSpec((None, 3, D_MODEL), lambda i: (row0 + row_step * (i // tiles_per_seq), 0, 0))


def _resident(shape):
    return pl.BlockSpec(shape, lambda *_: (0,) * len(shape), pipeline_mode=pl.Buffered(1))


def _conv_layer(x, mod, row0, row_step, seq_len, g, w_in, dw_w, dw_b, ln_g, ln_b, w_out):
    n_tok, d = x.shape
    tile = CONV_TILE
    seg = SEG_LEN
    segs_per_seq = seq_len // seg
    ext_halo = segs_per_seq > N_SEG
    tiles_per_seq = max(segs_per_seq // N_SEG, 1)
    assert (segs_per_seq % N_SEG == 0) if ext_halo else (N_SEG % segs_per_seq == 0)
    hb = tile // CONV_HALO
    n_hblk = n_tok // CONV_HALO
    tok_spec = pl.BlockSpec((tile, d), lambda i: (i, 0))
    in_specs = [tok_spec]
    args = [x]
    if ext_halo:
        in_specs += [pl.BlockSpec((CONV_HALO, d), lambda i: (jnp.maximum(i * hb - 1, 0), 0)),
                     pl.BlockSpec((CONV_HALO, d), lambda i: (jnp.minimum((i + 1) * hb, n_hblk - 1), 0))]
        args += [x, x]
    in_specs += [_mod_spec(row0, row_step, tiles_per_seq),
                 _resident((1, d)), _resident((d, 3 * d)), _resident((CONV_K, d)),
                 _resident((1, d)), _resident((1, d)), _resident((1, d)), _resident((d, d)),
                 _resident((tile, tile)), _resident((tile, tile))]
    perm = _segment_permutation(tile)
    args += [mod, g, w_in, dw_w, dw_b, ln_g, ln_b, w_out, jnp.asarray(perm, BF16), jnp.asarray(perm.T, BF16)]
    h_rows = tile + 2 * CONV_HALO if ext_halo else tile
    return pl.pallas_call(
        functools.partial(_conv_body, tile=tile, ext_halo=ext_halo, tiles_per_seq=tiles_per_seq,
                          segs_per_seq=segs_per_seq),
        grid=(n_tok // tile,),
        in_specs=in_specs,
        out_specs=tok_spec,
        out_shape=jax.ShapeDtypeStruct((n_tok, d), F32),
        scratch_shapes=[pltpu.VMEM((h_rows, d), BF16),
                        pltpu.VMEM((tile + 2 * CONV_HALO * N_SEG, d), F32),
                        pltpu.VMEM((tile, d), F32),
                        pltpu.VMEM((tile, d), BF16),
                        pltpu.VMEM((tile, d), F32),
                        pltpu.VMEM(((CONV_K + 1) * N_SEG, d), F32),
                        pltpu.VMEM((tile, d), BF16)],
        compiler_params=_params(("arbitrary",)),
        name="conv_layer",
    )(*args)


def _head_masks():
    lane = lax.broadcasted_iota(jnp.int32, (1, 2 * HEAD_DIM), 1)
    first = lane < HEAD_DIM
    return first, jnp.logical_not(first)


def _na_ctx_body(*refs, final, n_prev):
    x_ref, mod_ref, g_ref, win_ref, wout_ref = refs[:5]
    fg_ref = refs[5] if final else None
    o_ref, k_ref, v_ref, q_s, kb_s, vb_s, sz_s, att_s = refs[-8:]
    d = D_MODEL
    if n_prev:
        kprev_ref, vprev_ref = refs[-10:-8]
        k_ref[0:n_prev] = kprev_ref[...]
        v_ref[0:n_prev] = vprev_ref[...]
    x = x_ref[...]
    gs = g_ref[...] * (1.0 + mod_ref[1:2, :])
    h = _modulate(x, gs, mod_ref[0:1, :])
    for cb in range(d // COL_CHUNK):
        cs = cb * COL_CHUNK
        q_s[:, cs:cs + COL_CHUNK] = (_dot(h, win_ref[:, cs:cs + COL_CHUNK]) * (HEAD_DIM ** -0.5)).astype(BF16)
        k = _dot(h, win_ref[:, d + cs:d + cs + COL_CHUNK])
        k_ref[n_prev, :, cs:cs + COL_CHUNK] = k
        kb_s[:, cs:cs + COL_CHUNK] = k.astype(BF16)
        v = _dot(h, win_ref[:, 2 * d + cs:2 * d + cs + COL_CHUNK])
        v_ref[n_prev, :, cs:cs + COL_CHUNK] = v
        vb_s[:, cs:cs + COL_CHUNK] = v.astype(BF16)
        sz_s[:, cs:cs + COL_CHUNK] = _silu(_dot(h, win_ref[:, 3 * d + cs:3 * d + cs + COL_CHUNK]))

    masks = _head_masks()
    for hp in range(N_HEADS // 2):
        ls = hp * 2 * HEAD_DIM
        q = q_s[:, ls:ls + 2 * HEAD_DIM]
        kp = kb_s[:, ls:ls + 2 * HEAD_DIM]
        vp = vb_s[:, ls:ls + 2 * HEAD_DIM]
        outs = []
        for hh in range(2):
            s = _dot_nt(jnp.where(masks[hh], q, jnp.zeros_like(q)), kp)
            p = jnp.exp(s - jnp.max(s, axis=-1, keepdims=True))
            l = jnp.sum(p, axis=-1, keepdims=True)
            outs.append(_dot(p.astype(BF16), vp) / l)
        att = jnp.where(masks[0], outs[0], outs[1])
        att_s[:, ls:ls + 2 * HEAD_DIM] = (att * sz_s[:, ls:ls + 2 * HEAD_DIM]).astype(BF16)

    y = x + mod_ref[2:3, :] * _dot(att_s[...], wout_ref[...])
    if final:
        y = _rms(y) * fg_ref[...]
    o_ref[...] = y


def _na_ctx_layer(x, mod, g, w_in, w_out, seq_len, final_g=None, prev=None):
    n_tok, d = x.shape
    n_seq = n_tok // seq_len
    n_prev = 0 if prev is None else prev[0].shape[1]
    const = lambda b: (0, 0)
    final = final_g is not None
    in_specs = [pl.BlockSpec((seq_len, d), lambda b: (b, 0)),
                pl.BlockSpec((None, 3, d), lambda b: (0, 0, 0)),
                pl.BlockSpec((1, d), const),
                pl.BlockSpec((d, 4 * d), const),
                pl.BlockSpec((d, d), const)]
    args = [x, mod, g, w_in, w_out]
    if final:
        in_specs.append(pl.BlockSpec((1, d), const))
        args.append(final_g)
    if n_prev:
        in_specs += [pl.BlockSpec((None, n_prev, seq_len, d), lambda b: (b, 0, 0, 0))] * 2
        args += list(prev)
    stack_spec = pl.BlockSpec((None, n_prev + 1, seq_len, d), lambda b: (b, 0, 0, 0))
    stack_shape = jax.ShapeDtypeStruct((n_seq, n_prev + 1, seq_len, d), F32)
    return pl.pallas_call(
        functools.partial(_na_ctx_body, final=final, n_prev=n_prev),
        grid=(n_seq,),
        in_specs=in_specs,
        out_specs=[pl.BlockSpec((seq_len, d), lambda b: (b, 0)), stack_spec, stack_spec],
        out_shape=[jax.ShapeDtypeStruct((n_tok, d), F32), stack_shape, stack_shape],
        scratch_shapes=[pltpu.VMEM((seq_len, d), BF16),
                        pltpu.VMEM((seq_len, d), BF16),
                        pltpu.VMEM((seq_len, d), BF16),
                        pltpu.VMEM((seq_len, d), F32),
                        pltpu.VMEM((seq_len, d), BF16)],
        compiler_params=_params(("arbitrary",)),
        name="na_ctx_layer",
    )(*args)


def _na_proj_body(x_ref, mod_ref, g_ref, win_ref, q_ref, k_ref, v_ref, sz_ref):
    d = D_MODEL
    gs = g_ref[...] * (1.0 + mod_ref[1:2, :])
    h = _modulate(x_ref[...], gs, mod_ref[0:1, :])
    for cb in range(d // COL_CHUNK):
        cs = cb * COL_CHUNK
        q_ref[:, cs:cs + COL_CHUNK] = (_dot(h, win_ref[:, cs:cs + COL_CHUNK]) * (HEAD_DIM ** -0.5 * LOG2_E)).astype(BF16)
        k_ref[:, cs:cs + COL_CHUNK] = _dot(h, win_ref[:, d + cs:d + cs + COL_CHUNK]).astype(BF16)
        v_ref[:, cs:cs + COL_CHUNK] = _dot(h, win_ref[:, 2 * d + cs:2 * d + cs + COL_CHUNK]).astype(BF16)
        sz_ref[:, cs:cs + COL_CHUNK] = _silu(_dot(h, win_ref[:, 3 * d + cs:3 * d + cs + COL_CHUNK])).astype(BF16)


def _na_proj(x, mod, row0, seq_len, tile, g, w_in):
    n_tok, d = x.shape
    tiles_per_seq = seq_len // tile
    const = lambda i: (0, 0)
    tok_spec = pl.BlockSpec((tile, d), lambda i: (i, 0))
    out = jax.ShapeDtypeStruct((n_tok, d), BF16)
    return pl.pallas_call(
        _na_proj_body,
        grid=(n_tok // tile,),
        in_specs=[tok_spec,
                  _mod_spec(row0, 1, tiles_per_seq),
                  pl.BlockSpec((1, d), const),
                  pl.BlockSpec((d, 4 * d), const)],
        out_specs=[tok_spec] * 4,
        out_shape=[out] * 4,
        compiler_params=_params(("arbitrary",)),
        name="na_proj",
    )(x, mod, g, w_in)


def _block_rows(qb):
    ks = min(max(Q_ROWS * qb - ROW_WIN // 2, 0), GRID_ROWS - K_ROWS)
    rs = [min(max(Q_ROWS * qb + qr - ROW_WIN // 2, 0), GRID_ROWS - ROW_WIN) for qr in range(Q_ROWS)]
    return ks, rs


_CLASS_BLOCKS = (0, 2, N_QBLK - 1)


def _na_att_body(q_ref, k_ref, v_ref, kc_ref, vc_ref, t2_ref, o_ref, bias_s, kc_s, vc_s):
    masks = _head_masks()
    first = masks[0]
    kc_s[...] = kc_ref[...].astype(BF16)

    def with_ones(v, hh):
        return jnp.where(masks[hh], v, jnp.ones_like(v))

    vc = vc_ref[...].astype(BF16)
    for hh in range(2):
        vc_s[hh] = with_ones(vc, hh)

    def tile_rows(qb, qr, m):
        ks, rs = _block_rows(qb)
        ka = ks + 2 * m
        va = rs[qr] <= ka < rs[qr] + ROW_WIN
        vb = rs[qr] <= ka + 1 < rs[qr] + ROW_WIN
        return va, vb, ka - (Q_ROWS * qb + qr) + ROW_WIN

    @pl.when(pl.program_id(1) == 0)
    def _():
        neg = jnp.full((GRID_W, 2 * GRID_W), NEG_INF, F32)
        for cls, qb in enumerate(_CLASS_BLOCKS):
            for hh in range(2):
                for qr in range(Q_ROWS):
                    for m in range(K_ROWS // 2):
                        va, vb, e = tile_rows(qb, qr, m)
                        if not (va or vb):
                            continue
                        t = t2_ref[hh, e]
                        if not va:
                            t = jnp.where(first, neg, t)
                        if not vb:
                            t = jnp.where(first, t, neg)
                        bias_s[cls * 2 + hh, qr * GRID_W:(qr + 1) * GRID_W,
                               m * 2 * GRID_W:(m + 1) * 2 * GRID_W] = t

    def block(q0, k0, cls):
        q = q_ref[pl.ds(q0, Q_BLK), :]
        kl = k_ref[pl.ds(k0, K_BLK), :]
        vl = v_ref[pl.ds(k0, K_BLK), :]
        qb = _CLASS_BLOCKS[cls]
        outs = []
        for hh in range(2):
            qm = jnp.where(masks[hh], q, jnp.zeros_like(q))
            sl = _dot_nt(qm, kl)
            sc = _dot_nt(qm, kc_s[...])
            p_l, p_c = [], []
            for qr in range(Q_ROWS):
                rows = slice(qr * GRID_W, (qr + 1) * GRID_W)
                tiles = {}
                for m in range(K_ROWS // 2):
                    va, vb, _ = tile_rows(qb, qr, m)
                    if va or vb:
                        cols = slice(m * 2 * GRID_W, (m + 1) * 2 * GRID_W)
                        tiles[m] = sl[rows, cols] + bias_s[cls * 2 + hh, rows, cols]
                s_ctx = sc[rows, :]
                mx = jnp.max(s_ctx, axis=-1, keepdims=True)
                for t in tiles.values():
                    mx = jnp.maximum(mx, jnp.max(t, axis=-1, keepdims=True))
                zero = jnp.zeros((GRID_W, 2 * GRID_W), BF16)
                p_l.append(jnp.concatenate(
                    [jnp.exp2(tiles[m] - mx).astype(BF16) if m in tiles else zero for m in range(K_ROWS // 2)],
                    axis=1))
                p_c.append(jnp.exp2(s_ctx - mx).astype(BF16))
            o = (_dot(jnp.concatenate(p_l, axis=0), with_ones(vl, hh))
                 + _dot(jnp.concatenate(p_c, axis=0), vc_s[hh]))
            outs.append(o / pltpu.roll(o, HEAD_DIM, 1))
        o_ref[pl.ds(q0, Q_BLK), :] = jnp.where(first, outs[0], outs[1]).astype(BF16)

    block(0, _block_rows(0)[0] * GRID_W, 0)
    block((N_QBLK - 1) * Q_BLK, _block_rows(N_QBLK - 1)[0] * GRID_W, 2)

    def interior(qb, carry):
        q0 = pl.multiple_of(qb * Q_BLK, Q_BLK)
        block(q0, pl.multiple_of(q0 - (ROW_WIN // 2) * GRID_W, Q_BLK), 1)
        return carry

    lax.fori_loop(1, N_QBLK - 1, interior, 0, unroll=3)


def _na_att(q, k, v, cache_k, cache_v, t2, layer_j, n_batch):
    n_tok, d = q.shape
    seq_len = n_tok // n_batch
    past = cache_k.shape[2]
    pw = 2 * HEAD_DIM
    tok_spec = pl.BlockSpec((seq_len, pw), lambda hp, b: (b, hp))
    cache_spec = pl.BlockSpec((None, None, past, pw), lambda hp, b: (b, layer_j, 0, hp))
    return pl.pallas_call(
        _na_att_body,
        grid=(N_HEADS // 2, n_batch),
        in_specs=[tok_spec, tok_spec, tok_spec, cache_spec, cache_spec,
                  pl.BlockSpec((None, 2, 2 * ROW_WIN, GRID_W, pw), lambda hp, b: (layer_j, hp, 0, 0, 0))],
        out_specs=tok_spec,
        out_shape=jax.ShapeDtypeStruct((n_tok, d), BF16),
        scratch_shapes=[pltpu.VMEM((3 * 2, Q_BLK, K_BLK), F32),
                        pltpu.VMEM((past, pw), BF16),
                        pltpu.VMEM((2, past, pw), BF16)],
        compiler_params=_params(("arbitrary", "arbitrary")),
        name="na_attention",
    )(q, k, v, cache_k, cache_v, t2)


def _na_out_body(*refs, final):
    if final:
        x_ref, att_ref, sz_ref, mod_ref, wout_ref, fg_ref, o_ref = refs
    else:
        x_ref, att_ref, sz_ref, mod_ref, wout_ref, o_ref = refs
    a = (att_ref[...].astype(F32) * sz_ref[...].astype(F32)).astype(BF16)
    y = x_ref[...] + mod_ref[2:3, :] * _dot(a, wout_ref[...])
    if final:
        y = _rms(y) * fg_ref[...]
    o_ref[...] = y


def _na_out(x, att, sz, mod, row0, seq_len, tile, w_out, final_g=None):
    n_tok, d = x.shape
    tiles_per_seq = seq_len // tile
    const = lambda i: (0, 0)
    tok_spec = pl.BlockSpec((tile, d), lambda i: (i, 0))
    final = final_g is not None
    in_specs = [tok_spec, tok_spec, tok_spec,
                _mod_spec(row0, 1, tiles_per_seq),
                pl.BlockSpec((d, d), const)]
    args = [x, att, sz, mod, w_out]
    if final:
        in_specs.append(pl.BlockSpec((1, d), const))
        args.append(final_g)
    return pl.pallas_call(
        functools.partial(_na_out_body, final=final),
        grid=(n_tok // tile,),
        in_specs=in_specs,
        out_specs=tok_spec,
        out_shape=jax.ShapeDtypeStruct((n_tok, d), F32),
        compiler_params=_params(("arbitrary",)),
        name="na_out",
    )(*args)


def _bias_tables(rpb):
    n_dr, n_dc = 2 * ROW_WIN - 1, 2 * COL_WIN - 1
    qc = np.arange(GRID_W)[:, None]
    lane = np.arange(2 * GRID_W)[None, :]
    kc, half = lane % GRID_W, lane // GRID_W
    start = np.clip(qc - COL_WIN // 2, 0, GRID_W - COL_WIN)
    in_win = (kc >= start) & (kc < start + COL_WIN)
    dc = kc - qc + COL_WIN - 1
    onehot = ((np.arange(n_dc)[:, None, None] == dc[None]) & in_win[None])[None] \
        & (np.arange(2)[:, None, None, None] == half[None, None])
    padded = jnp.pad(rpb, ((0, 0), (0, 0), (1, 1), (0, 0)))
    rows = jnp.stack([padded[:, :, :n_dr + 1], padded[:, :, 1:n_dr + 2]], axis=3)
    t = jnp.einsum("lhexj,xjqk->lheqk", rows, jnp.asarray(onehot, F32), precision=lax.Precision.HIGHEST)
    dr = np.arange(n_dr + 1)[:, None, None] - 1 + half[None]
    valid = (dr >= 0) & (dr < n_dr) & in_win[None]
    return jnp.where(jnp.asarray(valid), t * LOG2_E, NEG_INF)


NA_TILE = 1024


def kernel(x_prompt, x_sample, c, cache_k, cache_v, c_ctx, norm_g, ada_w, ada_b, conv_w_in, conv_dw_w,
           conv_dw_b, conv_ln_g, conv_ln_b, conv_w_out, na_w_in, na_rpb, na_w_out, final_g):
    n_ctx, ctx_len, d = x_prompt.shape
    n_dec, dec_len, _ = x_sample.shape
    n_na = na_w_in.shape[0]
    past = cache_k.shape[2]

    cvec = jnp.concatenate([c_ctx[None], c, jnp.zeros((MOD_ROWS - 1 - n_dec, d), F32)], axis=0)
    mod = _ada(cvec, ada_w, ada_b).reshape(DEPTH, MOD_ROWS, 3, d)

    xc = x_prompt.reshape(n_ctx * ctx_len, d)
    xs = x_sample.reshape(n_dec * dec_len, d)
    ck = cache_k.reshape(n_dec, n_na, past, d)
    cv = cache_v.reshape(n_dec, n_na, past, d)
    fg = final_g.reshape(1, d)
    bias_tables = _bias_tables(na_rpb)
    ctx_kv = None
    for i in range(DEPTH):
        j = i // 2
        g = norm_g[i].reshape(1, d)
        if i % 2 == 0:
            wi = conv_w_in[j].astype(BF16)
            wo = conv_w_out[j].astype(BF16)
            rest = (g, wi, conv_dw_w[j], conv_dw_b[j].reshape(1, d), conv_ln_g[j].reshape(1, d),
                    conv_ln_b[j].reshape(1, d), wo)
            xc = _conv_layer(xc, mod[i], 0, 0, ctx_len, *rest)
            xs = _conv_layer(xs, mod[i], 1, 1, dec_len, *rest)
        else:
            wi = na_w_in[j].astype(BF16)
            wo = na_w_out[j].astype(BF16)
            last = i == DEPTH - 1
            xc, *ctx_kv = _na_ctx_layer(xc, mod[i], g, wi, wo, ctx_len, final_g=fg if last else None, prev=ctx_kv)
            q, k, v, sz = _na_proj(xs, mod[i], 1, dec_len, NA_TILE, g, wi)
            att = _na_att(q, k, v, ck, cv, bias_tables, j, n_dec)
            xs = _na_out(xs, att, sz, mod[i], 1, dec_len, NA_TILE, wo, final_g=fg if last else None)

    y_prompt = xc.reshape(n_ctx, ctx_len, d)
    y_sample = xs.reshape(n_dec, dec_len, d)
    new_cache_k, new_cache_v = (t.reshape(n_ctx, n_na, ctx_len, N_HEADS, HEAD_DIM) for t in ctx_kv)
    return (y_prompt, y_sample, new_cache_k, new_cache_v)
```

```python
import functools

import numpy as np
import jax
import jax.numpy as jnp
from jax import lax
from jax.experimental import pallas as pl
from jax.experimental.pallas import tpu as pltpu

F32 = jnp.float32
BF16 = jnp.bfloat16

D_MODEL = 1024
LANES = 128
DEPTH = 4
N_HEADS = 16
HEAD_DIM = 64
GRID_W = 64
GRID_ROWS = 32
CONV_K = 31
CONV_HALO = 16
ROW_WIN = 8
COL_WIN = 16
EPS = 1e-6
NEG_INF = -1e30
LOG2_E = 1.4426950408889634
MOD_ROWS = 16

Q_ROWS = 4
Q_BLK = Q_ROWS * GRID_W
K_ROWS = 12
K_BLK = K_ROWS * GRID_W
N_QBLK = GRID_ROWS // Q_ROWS

VMEM_LIMIT_BYTES = 56 * 1024 * 1024


def _params(sem):
    return pltpu.CompilerParams(dimension_semantics=sem, vmem_limit_bytes=VMEM_LIMIT_BYTES)


def _sigmoid(x):
    return 1.0 / (1.0 + jnp.exp(-x))


def _silu(x):
    return x * _sigmoid(x)


def _rms(x):
    return x * lax.rsqrt(jnp.mean(x * x, axis=-1, keepdims=True) + EPS)


def _dot(a, b):
    return jnp.dot(a, b, preferred_element_type=F32)


def _dot_nt(a, b):
    return lax.dot_general(a, b, (((1,), (1,)), ((), ())), preferred_element_type=F32)


ADA_TILE = 768


def _ada_body(c_ref, w_ref, b_ref, o_ref):
    s = _silu(c_ref[...]).astype(BF16)
    o_ref[...] = _dot(s, w_ref[...].astype(BF16)) + b_ref[...]


def _ada(cvec, ada_w, ada_b):
    n_out = 3 * D_MODEL
    return pl.pallas_call(
        _ada_body,
        grid=(DEPTH, n_out // ADA_TILE),
        in_specs=[pl.BlockSpec((MOD_ROWS, D_MODEL), lambda l, n: (0, 0)),
                  pl.BlockSpec((None, D_MODEL, ADA_TILE), lambda l, n: (l, 0, n)),
                  pl.BlockSpec((None, 1, ADA_TILE), lambda l, n: (l, 0, n))],
        out_specs=pl.BlockSpec((None, MOD_ROWS, ADA_TILE), lambda l, n: (l, 0, n)),
        out_shape=jax.ShapeDtypeStruct((DEPTH, MOD_ROWS, n_out), F32),
        compiler_params=_params(("arbitrary", "arbitrary")),
        name="ada_params",
    )(cvec, ada_w, ada_b.reshape(DEPTH, 1, n_out))


def _modulate(x, gs, shift):
    return (_rms(x) * gs + shift).astype(BF16)


N_SEG = 8
CONV_TILE = 1024
SEG_LEN = CONV_TILE // N_SEG
PRE_ROWS = 64
ROW_CHUNK = 128
COL_W = 256
NORM_ROWS = 32
COL_CHUNK = 512


def _segment_permutation(tile):
    seg = tile // N_SEG
    p = np.zeros((tile, tile), np.float32)
    dst = np.arange(tile)
    p[dst, (dst % N_SEG) * seg + dst // N_SEG] = 1.0
    return p


def _conv_body(*refs, tile, ext_halo, tiles_per_seq, segs_per_seq):
    x_ref, refs = refs[0], refs[1:]
    if ext_halo:
        (xp_ref, xn_ref, mod_ref, g_ref, win_ref, dww_ref, dwb_ref, lng_ref, lnb_ref, wout_ref, perm_ref, unperm_ref,
         o_ref, h_s, u_s, sz_s, gt_s, v_s, wb_s, hn_s) = refs
    else:
        (mod_ref, g_ref, win_ref, dww_ref, dwb_ref, lng_ref, lnb_ref, wout_ref, perm_ref, unperm_ref,
         o_ref, h_s, u_s, sz_s, gt_s, v_s, wb_s, hn_s) = refs
    d = D_MODEL
    halo_rows = CONV_HALO * N_SEG
    main0 = halo_rows
    back0 = main0 + tile
    shift = mod_ref[0:1, :]
    gate = mod_ref[2:3, :]
    gs = g_ref[...] * (1.0 + mod_ref[1:2, :])

    def modulate_rows(c, carry):
        rows = pl.ds(pl.multiple_of(c * PRE_ROWS, PRE_ROWS), PRE_ROWS)
        hn_s[rows, :] = _modulate(x_ref[rows, :], gs, shift)
        return carry

    lax.fori_loop(0, tile // PRE_ROWS, modulate_rows, 0, unroll=2)
    for cb in range(d // COL_CHUNK):
        cols = slice(cb * COL_CHUNK, (cb + 1) * COL_CHUNK)
        h_s[0:tile, cols] = _dot(perm_ref[...], hn_s[:, cols]).astype(BF16)
    n_rc = tile // ROW_CHUNK
    n_cc = d // COL_W
    for k in range(CONV_K):
        wb_s[k * N_SEG:(k + 1) * N_SEG, :] = jnp.broadcast_to(dww_ref[k:k + 1, :], (N_SEG, d))
    wb_s[CONV_K * N_SEG:(CONV_K + 1) * N_SEG, :] = jnp.broadcast_to(dwb_ref[...], (N_SEG, d))

    def row0(r):
        return r * ROW_CHUNK if isinstance(r, int) else pl.multiple_of(r * ROW_CHUNK, ROW_CHUNK)

    sub = lax.broadcasted_iota(jnp.int32, (N_SEG, 1), 0)
    if ext_halo:
        i = pl.program_id(0)
        h_s[tile:tile + CONV_HALO, :] = _modulate(xp_ref[...], gs, shift)
        h_s[tile + CONV_HALO:tile + 2 * CONV_HALO, :] = _modulate(xn_ref[...], gs, shift)
        he = h_s[tile:tile + 2 * CONV_HALO, :]
        for cb in range(d // COL_CHUNK):
            cs = cb * COL_CHUNK
            a = _dot(he, win_ref[:, cs:cs + COL_CHUNK])
            b = _dot(he, win_ref[:, d + cs:d + cs + COL_CHUNK])
            u_s[0:2 * CONV_HALO, cs:cs + COL_CHUNK] = a * _sigmoid(b)
        starts_seq = sub == 0
        ends_seq = sub == N_SEG - 1
    else:
        starts_seq = sub % segs_per_seq == 0
        ends_seq = sub % segs_per_seq == segs_per_seq - 1

    def fill_halo(j):
        cols = slice(j * COL_W, (j + 1) * COL_W)
        if ext_halo:
            u_prev = jnp.where(i % tiles_per_seq == 0, 0.0, u_s[0:CONV_HALO, cols])
            u_next = jnp.where(i % tiles_per_seq == tiles_per_seq - 1, 0.0, u_s[CONV_HALO:2 * CONV_HALO, cols])
        front = u_s[back0 - halo_rows - 1:back0 - 1, cols]
        back = u_s[main0 + 1:main0 + 1 + halo_rows, cols]
        for p in range(CONV_HALO):
            rows = slice(p * N_SEG, (p + 1) * N_SEG)
            edge_f = jnp.broadcast_to(u_prev[p:p + 1, :], (N_SEG, COL_W)) if ext_halo else 0.0
            edge_b = jnp.broadcast_to(u_next[p:p + 1, :], (N_SEG, COL_W)) if ext_halo else 0.0
            u_s[p * N_SEG:(p + 1) * N_SEG, cols] = jnp.where(starts_seq, edge_f, front[rows])
            u_s[back0 + p * N_SEG:back0 + (p + 1) * N_SEG, cols] = jnp.where(ends_seq, edge_b, back[rows])

    def conv_chunk(j, r):
        r0 = row0(r)
        for cb in range(j * COL_W // LANES, (j + 1) * COL_W // LANES):
            lanes = slice(cb * LANES, (cb + 1) * LANES)
            n_pos = ROW_CHUNK // N_SEG
            accs = [wb_s[CONV_K * N_SEG:(CONV_K + 1) * N_SEG, lanes]] * n_pos
            for k in range(CONV_K):
                wk = wb_s[k * N_SEG:(k + 1) * N_SEG, lanes]
                k0 = r0 + (k + CONV_HALO - CONV_K // 2) * N_SEG
                accs = [acc + u_s[pl.ds(k0 + p * N_SEG, N_SEG), lanes] * wk for p, acc in enumerate(accs)]
            v_s[pl.ds(r0, ROW_CHUNK), lanes] = jnp.concatenate(accs, axis=0)

    for cb in range(d // COL_CHUNK):
        cs = cb * COL_CHUNK
        hm = h_s[0:tile, :]
        a = _dot(hm, win_ref[:, cs:cs + COL_CHUNK])
        b = _dot(hm, win_ref[:, d + cs:d + cs + COL_CHUNK])
        u_s[main0:main0 + tile, cs:cs + COL_CHUNK] = a * _sigmoid(b)
        z = _dot(hm, win_ref[:, 2 * d + cs:2 * d + cs + COL_CHUNK])
        sz_s[:, cs:cs + COL_CHUNK] = _silu(z)
    for j in range(n_cc):
        fill_halo(j)

    def conv_rows(r, carry):
        for j in range(n_cc):
            conv_chunk(j, r)
        return carry

    def norm_gate(r):
        for part in range(ROW_CHUNK // NORM_ROWS):
            rows = pl.ds(row0(r) + part * NORM_ROWS, NORM_ROWS)
            acc = v_s[rows, :]
            mu = jnp.mean(acc, axis=-1, keepdims=True)
            dev = acc - mu
            var = jnp.mean(dev * dev, axis=-1, keepdims=True)
            y = dev * lax.rsqrt(var + EPS) * lng_ref[...] + lnb_ref[...]
            gt_s[rows, :] = (_silu(y) * sz_s[rows, :]).astype(BF16)

    def norm_rows(r, carry):
        norm_gate(r)
        return carry

    lax.fori_loop(0, n_rc, conv_rows, 0)
    lax.fori_loop(0, n_rc, norm_rows, 0)
    for cb in range(d // COL_CHUNK):
        cols = slice(cb * COL_CHUNK, (cb + 1) * COL_CHUNK)
        hn_s[:, cols] = _dot(unperm_ref[...], gt_s[:, cols]).astype(BF16)
    o_ref[...] = x_ref[...] + gate * _dot(hn_s[...], wout_ref[...])


def _mod_spec(row0, row_step, tiles_per_seq):
    return pl.BlockSpec((None, 3, D_MODEL), lambda i: (row0 + row_step * (i // tiles_per_seq), 0, 0))


def _resident(shape):
    return pl.BlockSpec(shape, lambda *_: (0,) * len(shape), pipeline_mode=pl.Buffered(1))


def _conv_layer(x, mod, row0, row_step, seq_len, g, w_in, dw_w, dw_b, ln_g, ln_b, w_out):
    n_tok, d = x.shape
    tile = CONV_TILE
    seg = SEG_LEN
    segs_per_seq = seq_len // seg
    ext_halo = segs_per_seq > N_SEG
    tiles_per_seq = max(segs_per_seq // N_SEG, 1)
    assert (segs_per_seq % N_SEG == 0) if ext_halo else (N_SEG % segs_per_seq == 0)
    hb = tile // CONV_HALO
    n_hblk = n_tok // CONV_HALO
    tok_spec = pl.BlockSpec((tile, d), lambda i: (i, 0))
    in_specs = [tok_spec]
    args = [x]
    if ext_halo:
        in_specs += [pl.BlockSpec((CONV_HALO, d), lambda i: (jnp.maximum(i * hb - 1, 0), 0)),
                     pl.BlockSpec((CONV_HALO, d), lambda i: (jnp.minimum((i + 1) * hb, n_hblk - 1), 0))]
        args += [x, x]
    in_specs += [_mod_spec(row0, row_step, tiles_per_seq),
                 _resident((1, d)), _resident((d, 3 * d)), _resident((CONV_K, d)),
                 _resident((1, d)), _resident((1, d)), _resident((1, d)), _resident((d, d)),
                 _resident((tile, tile)), _resident((tile, tile))]
    perm = _segment_permutation(tile)
    args += [mod, g, w_in, dw_w, dw_b, ln_g, ln_b, w_out, jnp.asarray(perm, BF16), jnp.asarray(perm.T, BF16)]
    h_rows = tile + 2 * CONV_HALO if ext_halo else tile
    return pl.pallas_call(
        functools.partial(_conv_body, tile=tile, ext_halo=ext_halo, tiles_per_seq=tiles_per_seq,
                          segs_per_seq=segs_per_seq),
        grid=(n_tok // tile,),
        in_specs=in_specs,
        out_specs=tok_spec,
        out_shape=jax.ShapeDtypeStruct((n_tok, d), F32),
        scratch_shapes=[pltpu.VMEM((h_rows, d), BF16),
                        pltpu.VMEM((tile + 2 * CONV_HALO * N_SEG, d), F32),
                        pltpu.VMEM((tile, d), F32),
                        pltpu.VMEM((tile, d), BF16),
                        pltpu.VMEM((tile, d), F32),
                        pltpu.VMEM(((CONV_K + 1) * N_SEG, d), F32),
                        pltpu.VMEM((tile, d), BF16)],
        compiler_params=_params(("arbitrary",)),
        name="conv_layer",
    )(*args)


CTX_SEQS = 2


def _head_masks():
    lane = lax.broadcasted_iota(jnp.int32, (1, 2 * HEAD_DIM), 1)
    first = lane < HEAD_DIM
    return first, jnp.logical_not(first)


def _na_ctx_body(*refs, final, n_prev, seq_len):
    x_ref, mod_ref, g_ref, win_ref, wout_ref = refs[:5]
    fg_ref = refs[5] if final else None
    o_ref, k_ref, v_ref, q_s, kb_s, vb_s, sz_s, att_s = refs[-8:]
    d = D_MODEL
    if n_prev:
        kprev_ref, vprev_ref = refs[-10:-8]
        k_ref[:, 0:n_prev] = kprev_ref[...]
        v_ref[:, 0:n_prev] = vprev_ref[...]
    x = x_ref[...]
    gs = g_ref[...] * (1.0 + mod_ref[1:2, :])
    h = _modulate(x, gs, mod_ref[0:1, :])
    for cb in range(d // COL_CHUNK):
        cs = cb * COL_CHUNK
        q_s[:, cs:cs + COL_CHUNK] = (_dot(h, win_ref[:, cs:cs + COL_CHUNK]) * (HEAD_DIM ** -0.5)).astype(BF16)
        k = _dot(h, win_ref[:, d + cs:d + cs + COL_CHUNK])
        v = _dot(h, win_ref[:, 2 * d + cs:2 * d + cs + COL_CHUNK])
        for s in range(CTX_SEQS):
            k_ref[s, n_prev, :, cs:cs + COL_CHUNK] = k[s * seq_len:(s + 1) * seq_len]
            v_ref[s, n_prev, :, cs:cs + COL_CHUNK] = v[s * seq_len:(s + 1) * seq_len]
        kb_s[:, cs:cs + COL_CHUNK] = k.astype(BF16)
        vb_s[:, cs:cs + COL_CHUNK] = v.astype(BF16)
        sz_s[:, cs:cs + COL_CHUNK] = _silu(_dot(h, win_ref[:, 3 * d + cs:3 * d + cs + COL_CHUNK]))

    masks = _head_masks()
    for s in range(CTX_SEQS):
        rows = slice(s * seq_len, (s + 1) * seq_len)
        for hp in range(N_HEADS // 2):
            ls = hp * 2 * HEAD_DIM
            q = q_s[rows, ls:ls + 2 * HEAD_DIM]
            kp = kb_s[rows, ls:ls + 2 * HEAD_DIM]
            vp = vb_s[rows, ls:ls + 2 * HEAD_DIM]
            outs = []
            for hh in range(2):
                sc = _dot_nt(jnp.where(masks[hh], q, jnp.zeros_like(q)), kp)
                p = jnp.exp(sc - jnp.max(sc, axis=-1, keepdims=True))
                l = jnp.sum(p, axis=-1, keepdims=True)
                outs.append(_dot(p.astype(BF16), vp) / l)
            att = jnp.where(masks[0], outs[0], outs[1])
            att_s[rows, ls:ls + 2 * HEAD_DIM] = (att * sz_s[rows, ls:ls + 2 * HEAD_DIM]).astype(BF16)

    y = x + mod_ref[2:3, :] * _dot(att_s[...], wout_ref[...])
    if final:
        y = _rms(y) * fg_ref[...]
    o_ref[...] = y


def _na_ctx_layer(x, mod, g, w_in, w_out, seq_len, final_g=None, prev=None):
    n_tok, d = x.shape
    n_seq = n_tok // seq_len
    rows = CTX_SEQS * seq_len
    n_prev = 0 if prev is None else prev[0].shape[1]
    const = lambda b: (0, 0)
    final = final_g is not None
    tok_spec = pl.BlockSpec((rows, d), lambda b: (b, 0))
    in_specs = [tok_spec,
                pl.BlockSpec((None, 3, d), lambda b: (0, 0, 0)),
                pl.BlockSpec((1, d), const),
                pl.BlockSpec((d, 4 * d), const),
                pl.BlockSpec((d, d), const)]
    args = [x, mod, g, w_in, w_out]
    if final:
        in_specs.append(pl.BlockSpec((1, d), const))
        args.append(final_g)
    if n_prev:
        in_specs += [pl.BlockSpec((CTX_SEQS, n_prev, seq_len, d), lambda b: (b, 0, 0, 0))] * 2
        args += list(prev)
    stack_spec = pl.BlockSpec((CTX_SEQS, n_prev + 1, seq_len, d), lambda b: (b, 0, 0, 0))
    stack_shape = jax.ShapeDtypeStruct((n_seq, n_prev + 1, seq_len, d), F32)
    return pl.pallas_call(
        functools.partial(_na_ctx_body, final=final, n_prev=n_prev, seq_len=seq_len),
        grid=(n_seq // CTX_SEQS,),
        in_specs=in_specs,
        out_specs=[tok_spec, stack_spec, stack_spec],
        out_shape=[jax.ShapeDtypeStruct((n_tok, d), F32), stack_shape, stack_shape],
        scratch_shapes=[pltpu.VMEM((rows, d), BF16),
                        pltpu.VMEM((rows, d), BF16),
                        pltpu.VMEM((rows, d), BF16),
                        pltpu.VMEM((rows, d), F32),
                        pltpu.VMEM((rows, d), BF16)],
        compiler_params=_params(("arbitrary",)),
        name="na_ctx_layer",
    )(*args)


def _na_proj_body(x_ref, mod_ref, g_ref, win_ref, q_ref, k_ref, v_ref, sz_ref):
    d = D_MODEL
    gs = g_ref[...] * (1.0 + mod_ref[1:2, :])
    h = _modulate(x_ref[...], gs, mod_ref[0:1, :])
    for cb in range(d // COL_CHUNK):
        cs = cb * COL_CHUNK
        q_ref[:, cs:cs + COL_CHUNK] = (_dot(h, win_ref[:, cs:cs + COL_CHUNK]) * (HEAD_DIM ** -0.5 * LOG2_E)).astype(BF16)
        k_ref[:, cs:cs + COL_CHUNK] = _dot(h, win_ref[:, d + cs:d + cs + COL_CHUNK]).astype(BF16)
        v_ref[:, cs:cs + COL_CHUNK] = _dot(h, win_ref[:, 2 * d + cs:2 * d + cs + COL_CHUNK]).astype(BF16)
        sz_ref[:, cs:cs + COL_CHUNK] = _silu(_dot(h, win_ref[:, 3 * d + cs:3 * d + cs + COL_CHUNK])).astype(BF16)


def _na_proj(x, mod, row0, seq_len, tile, g, w_in):
    n_tok, d = x.shape
    tiles_per_seq = seq_len // tile
    const = lambda i: (0, 0)
    tok_spec = pl.BlockSpec((tile, d), lambda i: (i, 0))
    out = jax.ShapeDtypeStruct((n_tok, d), BF16)
    return pl.pallas_call(
        _na_proj_body,
        grid=(n_tok // tile,),
        in_specs=[tok_spec,
                  _mod_spec(row0, 1, tiles_per_seq),
                  pl.BlockSpec((1, d), const),
                  pl.BlockSpec((d, 4 * d), const)],
        out_specs=[tok_spec] * 4,
        out_shape=[out] * 4,
        compiler_params=_params(("arbitrary",)),
        name="na_proj",
    )(x, mod, g, w_in)


def _block_rows(qb):
    ks = min(max(Q_ROWS * qb - ROW_WIN // 2, 0), GRID_ROWS - K_ROWS)
    rs = [min(max(Q_ROWS * qb + qr - ROW_WIN // 2, 0), GRID_ROWS - ROW_WIN) for qr in range(Q_ROWS)]
    return ks, rs


_CLASS_BLOCKS = (0, 2, N_QBLK - 1)


def _na_att_body(q_ref, k_ref, v_ref, sz_ref, kc_ref, vc_ref, t2_ref, o_ref, bias_s, kc_s, vc_s):
    masks = _head_masks()
    first = masks[0]
    kc_s[...] = kc_ref[...].astype(BF16)

    def with_ones(v, hh):
        return jnp.where(masks[hh], v, jnp.ones_like(v))

    vc = vc_ref[...].astype(BF16)
    for hh in range(2):
        vc_s[hh] = with_ones(vc, hh)

    def tile_rows(qb, qr, m):
        ks, rs = _block_rows(qb)
        ka = ks + 2 * m
        va = rs[qr] <= ka < rs[qr] + ROW_WIN
        vb = rs[qr] <= ka + 1 < rs[qr] + ROW_WIN
        return va, vb, ka - (Q_ROWS * qb + qr) + ROW_WIN

    @pl.when(pl.program_id(1) == 0)
    def _():
        neg = jnp.full((GRID_W, 2 * GRID_W), NEG_INF, F32)
        for cls, qb in enumerate(_CLASS_BLOCKS):
            for hh in range(2):
                for qr in range(Q_ROWS):
                    for m in range(K_ROWS // 2):
                        va, vb, e = tile_rows(qb, qr, m)
                        if not (va or vb):
                            continue
                        t = t2_ref[hh, e]
                        if not va:
                            t = jnp.where(first, neg, t)
                        if not vb:
                            t = jnp.where(first, t, neg)
                        bias_s[cls * 2 + hh, qr * GRID_W:(qr + 1) * GRID_W,
                               m * 2 * GRID_W:(m + 1) * 2 * GRID_W] = t

    def block(q0, k0, cls):
        q = q_ref[pl.ds(q0, Q_BLK), :]
        kl = k_ref[pl.ds(k0, K_BLK), :]
        vl = v_ref[pl.ds(k0, K_BLK), :]
        qb = _CLASS_BLOCKS[cls]
        outs = []
        for hh in range(2):
            qm = jnp.where(masks[hh], q, jnp.zeros_like(q))
            sl = _dot_nt(qm, kl)
            sc = _dot_nt(qm, kc_s[...])
            p_l, p_c = [], []
            for qr in range(Q_ROWS):
                rows = slice(qr * GRID_W, (qr + 1) * GRID_W)
                tiles = {}
                for m in range(K_ROWS // 2):
                    va, vb, _ = tile_rows(qb, qr, m)
                    if va or vb:
                        cols = slice(m * 2 * GRID_W, (m + 1) * 2 * GRID_W)
                        tiles[m] = sl[rows, cols] + bias_s[cls * 2 + hh, rows, cols]
                s_ctx = sc[rows, :]
                mx = jnp.max(s_ctx, axis=-1, keepdims=True)
                for t in tiles.values():
                    mx = jnp.maximum(mx, jnp.max(t, axis=-1, keepdims=True))
                zero = jnp.zeros((GRID_W, 2 * GRID_W), BF16)
                p_l.append(jnp.concatenate(
                    [jnp.exp2(tiles[m] - mx).astype(BF16) if m in tiles else zero for m in range(K_ROWS // 2)],
                    axis=1))
                p_c.append(jnp.exp2(s_ctx - mx).astype(BF16))
            o = (_dot(jnp.concatenate(p_l, axis=0), with_ones(vl, hh))
                 + _dot(jnp.concatenate(p_c, axis=0), vc_s[hh]))
            outs.append(o / pltpu.roll(o, HEAD_DIM, 1))
        att = jnp.where(first, outs[0], outs[1])
        o_ref[pl.ds(q0, Q_BLK), :] = (att * sz_ref[pl.ds(q0, Q_BLK), :].astype(F32)).astype(BF16)

    block(0, _block_rows(0)[0] * GRID_W, 0)
    block((N_QBLK - 1) * Q_BLK, _block_rows(N_QBLK - 1)[0] * GRID_W, 2)

    def interior(qb, carry):
        q0 = pl.multiple_of(qb * Q_BLK, Q_BLK)
        block(q0, pl.multiple_of(q0 - (ROW_WIN // 2) * GRID_W, Q_BLK), 1)
        return carry

    lax.fori_loop(1, N_QBLK - 1, interior, 0, unroll=3)


def _na_att(q, k, v, sz, cache_k, cache_v, t2, layer_j, n_batch):
    n_tok, d = q.shape
    seq_len = n_tok // n_batch
    past = cache_k.shape[2]
    pw = 2 * HEAD_DIM
    tok_spec = pl.BlockSpec((seq_len, pw), lambda hp, b: (b, hp))
    cache_spec = pl.BlockSpec((None, None, past, pw), lambda hp, b: (b, layer_j, 0, hp))
    return pl.pallas_call(
        _na_att_body,
        grid=(N_HEADS // 2, n_batch),
        in_specs=[tok_spec, tok_spec, tok_spec, tok_spec, cache_spec, cache_spec,
                  pl.BlockSpec((None, 2, 2 * ROW_WIN, GRID_W, pw), lambda hp, b: (layer_j, hp, 0, 0, 0))],
        out_specs=tok_spec,
        out_shape=jax.ShapeDtypeStruct((n_tok, d), BF16),
        scratch_shapes=[pltpu.VMEM((3 * 2, Q_BLK, K_BLK), F32),
                        pltpu.VMEM((past, pw), BF16),
                        pltpu.VMEM((2, past, pw), BF16)],
        compiler_params=_params(("arbitrary", "arbitrary")),
        name="na_attention",
    )(q, k, v, sz, cache_k, cache_v, t2)


def _na_out_body(*refs, final):
    if final:
        x_ref, att_ref, mod_ref, wout_ref, fg_ref, o_ref = refs
    else:
        x_ref, att_ref, mod_ref, wout_ref, o_ref = refs
    y = x_ref[...] + mod_ref[2:3, :] * _dot(att_ref[...], wout_ref[...])
    if final:
        y = _rms(y) * fg_ref[...]
    o_ref[...] = y


def _na_out(x, att, mod, row0, seq_len, tile, w_out, final_g=None):
    n_tok, d = x.shape
    tiles_per_seq = seq_len // tile
    const = lambda i: (0, 0)
    tok_spec = pl.BlockSpec((tile, d), lambda i: (i, 0))
    final = final_g is not None
    in_specs = [tok_spec, tok_spec,
                _mod_spec(row0, 1, tiles_per_seq),
                pl.BlockSpec((d, d), const)]
    args = [x, att, mod, w_out]
    if final:
        in_specs.append(pl.BlockSpec((1, d), const))
        args.append(final_g)
    return pl.pallas_call(
        functools.partial(_na_out_body, final=final),
        grid=(n_tok // tile,),
        in_specs=in_specs,
        out_specs=tok_spec,
        out_shape=jax.ShapeDtypeStruct((n_tok, d), F32),
        compiler_params=_params(("arbitrary",)),
        name="na_out",
    )(*args)


def _bias_tables(rpb):
    n_dr, n_dc = 2 * ROW_WIN - 1, 2 * COL_WIN - 1
    qc = np.arange(GRID_W)[:, None]
    lane = np.arange(2 * GRID_W)[None, :]
    kc, half = lane % GRID_W, lane // GRID_W
    start = np.clip(qc - COL_WIN // 2, 0, GRID_W - COL_WIN)
    in_win = (kc >= start) & (kc < start + COL_WIN)
    dc = kc - qc + COL_WIN - 1
    onehot = ((np.arange(n_dc)[:, None, None] == dc[None]) & in_win[None])[None] \
        & (np.arange(2)[:, None, None, None] == half[None, None])
    padded = jnp.pad(rpb, ((0, 0), (0, 0), (1, 1), (0, 0)))
    rows = jnp.stack([padded[:, :, :n_dr + 1], padded[:, :, 1:n_dr + 2]], axis=3)
    t = jnp.einsum("lhexj,xjqk->lheqk", rows, jnp.asarray(onehot, F32), precision=lax.Precision.HIGHEST)
    dr = np.arange(n_dr + 1)[:, None, None] - 1 + half[None]
    valid = (dr >= 0) & (dr < n_dr) & in_win[None]
    return jnp.where(jnp.asarray(valid), t * LOG2_E, NEG_INF)


NA_TILE = 1024


def kernel(x_prompt, x_sample, c, cache_k, cache_v, c_ctx, norm_g, ada_w, ada_b, conv_w_in, conv_dw_w,
           conv_dw_b, conv_ln_g, conv_ln_b, conv_w_out, na_w_in, na_rpb, na_w_out, final_g):
    n_ctx, ctx_len, d = x_prompt.shape
    n_dec, dec_len, _ = x_sample.shape
    n_na = na_w_in.shape[0]
    past = cache_k.shape[2]

    cvec = jnp.concatenate([c_ctx[None], c, jnp.zeros((MOD_ROWS - 1 - n_dec, d), F32)], axis=0)
    mod = _ada(cvec, ada_w, ada_b).reshape(DEPTH, MOD_ROWS, 3, d)

    xc = x_prompt.reshape(n_ctx * ctx_len, d)
    xs = x_sample.reshape(n_dec * dec_len, d)
    ck = cache_k.reshape(n_dec, n_na, past, d)
    cv = cache_v.reshape(n_dec, n_na, past, d)
    fg = final_g.reshape(1, d)
    bias_tables = _bias_tables(na_rpb)
    ctx_kv = None
    for i in range(DEPTH):
        j = i // 2
        g = norm_g[i].reshape(1, d)
        if i % 2 == 0:
            wi = conv_w_in[j].astype(BF16)
            wo = conv_w_out[j].astype(BF16)
            rest = (g, wi, conv_dw_w[j], conv_dw_b[j].reshape(1, d), conv_ln_g[j].reshape(1, d),
                    conv_ln_b[j].reshape(1, d), wo)
            xc = _conv_layer(xc, mod[i], 0, 0, ctx_len, *rest)
            xs = _conv_layer(xs, mod[i], 1, 1, dec_len, *rest)
        else:
            wi = na_w_in[j].astype(BF16)
            wo = na_w_out[j].astype(BF16)
            last = i == DEPTH - 1
            xc, *ctx_kv = _na_ctx_layer(xc, mod[i], g, wi, wo, ctx_len, final_g=fg if last else None, prev=ctx_kv)
            q, k, v, sz = _na_proj(xs, mod[i], 1, dec_len, NA_TILE, g, wi)
            att = _na_att(q, k, v, sz, ck, cv, bias_tables, j, n_dec)
            xs = _na_out(xs, att, mod[i], 1, dec_len, NA_TILE, wo, final_g=fg if last else None)

    y_prompt = xc.reshape(n_ctx, ctx_len, d)
    y_sample = xs.reshape(n_dec, dec_len, d)
    new_cache_k, new_cache_v = (t.reshape(n_ctx, n_na, ctx_len, N_HEADS, HEAD_DIM) for t in ctx_kv)
    return (y_prompt, y_sample, new_cache_k, new_cache_v)
```

```python
import functools

import numpy as np
import jax
import jax.numpy as jnp
from jax import lax
from jax.experimental import pallas as pl
from jax.experimental.pallas import tpu as pltpu

F32 = jnp.float32
BF16 = jnp.bfloat16

D_MODEL = 1024
LANES = 128
DEPTH = 4
N_HEADS = 16
HEAD_DIM = 64
GRID_W = 64
GRID_ROWS = 32
CONV_K = 31
CONV_HALO = 16
ROW_WIN = 8
COL_WIN = 16
EPS = 1e-6
NEG_INF = -1e30
LOG2_E = 1.4426950408889634
MOD_ROWS = 16

Q_ROWS = 4
Q_BLK = Q_ROWS * GRID_W
K_ROWS = 12
K_BLK = K_ROWS * GRID_W
N_QBLK = GRID_ROWS // Q_ROWS

VMEM_LIMIT_BYTES = 56 * 1024 * 1024


def _params(sem):
    return pltpu.CompilerParams(dimension_semantics=sem, vmem_limit_bytes=VMEM_LIMIT_BYTES)


def _sigmoid(x):
    return 1.0 / (1.0 + jnp.exp(-x))


def _silu(x):
    return x * _sigmoid(x)


def _rms(x):
    return x * lax.rsqrt(jnp.mean(x * x, axis=-1, keepdims=True) + EPS)


def _dot(a, b):
    return jnp.dot(a, b, preferred_element_type=F32)


def _dot_nt(a, b):
    return lax.dot_general(a, b, (((1,), (1,)), ((), ())), preferred_element_type=F32)


ADA_TILE = 768


def _ada_body(c_ref, w_ref, b_ref, o_ref):
    s = _silu(c_ref[...]).astype(BF16)
    o_ref[...] = _dot(s, w_ref[...].astype(BF16)) + b_ref[...]


def _ada(cvec, ada_w, ada_b):
    n_out = 3 * D_MODEL
    return pl.pallas_call(
        _ada_body,
        grid=(DEPTH, n_out // ADA_TILE),
        in_specs=[pl.BlockSpec((MOD_ROWS, D_MODEL), lambda l, n: (0, 0)),
                  pl.BlockSpec((None, D_MODEL, ADA_TILE), lambda l, n: (l, 0, n)),
                  pl.BlockSpec((None, 1, ADA_TILE), lambda l, n: (l, 0, n))],
        out_specs=pl.BlockSpec((None, MOD_ROWS, ADA_TILE), lambda l, n: (l, 0, n)),
        out_shape=jax.ShapeDtypeStruct((DEPTH, MOD_ROWS, n_out), F32),
        compiler_params=_params(("arbitrary", "arbitrary")),
        name="ada_params",
    )(cvec, ada_w, ada_b.reshape(DEPTH, 1, n_out))


def _modulate(x, gs, shift):
    return (_rms(x) * gs + shift).astype(BF16)


N_SEG = 8
CONV_TILE = 1024
SEG_LEN = CONV_TILE // N_SEG
PRE_ROWS = 64
ROW_CHUNK = 128
COL_W = 256
NORM_ROWS = 32
COL_CHUNK = 512


def _segment_permutation(tile):
    seg = tile // N_SEG
    p = np.zeros((tile, tile), np.float32)
    dst = np.arange(tile)
    p[dst, (dst % N_SEG) * seg + dst // N_SEG] = 1.0
    return p


def _conv_body(*refs, tile, ext_halo, tiles_per_seq, segs_per_seq):
    x_ref, refs = refs[0], refs[1:]
    if ext_halo:
        (xp_ref, xn_ref, mod_ref, g_ref, win_ref, dww_ref, dwb_ref, lng_ref, lnb_ref, wout_ref, perm_ref, unperm_ref,
         o_ref, h_s, u_s, sz_s, gt_s, v_s, wb_s, hn_s) = refs
    else:
        (mod_ref, g_ref, win_ref, dww_ref, dwb_ref, lng_ref, lnb_ref, wout_ref, perm_ref, unperm_ref,
         o_ref, h_s, u_s, sz_s, gt_s, v_s, wb_s, hn_s) = refs
    d = D_MODEL
    halo_rows = CONV_HALO * N_SEG
    main0 = halo_rows
    back0 = main0 + tile
    shift = mod_ref[0:1, :]
    gate = mod_ref[2:3, :]
    gs = g_ref[...] * (1.0 + mod_ref[1:2, :])

    def modulate_rows(c, carry):
        rows = pl.ds(pl.multiple_of(c * PRE_ROWS, PRE_ROWS), PRE_ROWS)
        hn_s[rows, :] = _modulate(x_ref[rows, :], gs, shift)
        return carry

    lax.fori_loop(0, tile // PRE_ROWS, modulate_rows, 0, unroll=4)
    for cb in range(d // COL_CHUNK):
        cols = slice(cb * COL_CHUNK, (cb + 1) * COL_CHUNK)
        h_s[0:tile, cols] = _dot(perm_ref[...], hn_s[:, cols]).astype(BF16)
    n_rc = tile // ROW_CHUNK
    n_cc = d // COL_W
    for k in range(CONV_K):
        wb_s[k * N_SEG:(k + 1) * N_SEG, :] = jnp.broadcast_to(dww_ref[k:k + 1, :], (N_SEG, d))
    wb_s[CONV_K * N_SEG:(CONV_K + 1) * N_SEG, :] = jnp.broadcast_to(dwb_ref[...], (N_SEG, d))

    def row0(r):
        return r * ROW_CHUNK if isinstance(r, int) else pl.multiple_of(r * ROW_CHUNK, ROW_CHUNK)

    sub = lax.broadcasted_iota(jnp.int32, (N_SEG, 1), 0)
    if ext_halo:
        i = pl.program_id(0)
        h_s[tile:tile + CONV_HALO, :] = _modulate(xp_ref[...], gs, shift)
        h_s[tile + CONV_HALO:tile + 2 * CONV_HALO, :] = _modulate(xn_ref[...], gs, shift)
        he = h_s[tile:tile + 2 * CONV_HALO, :]
        for cb in range(d // COL_CHUNK):
            cs = cb * COL_CHUNK
            a = _dot(he, win_ref[:, cs:cs + COL_CHUNK])
            b = _dot(he, win_ref[:, d + cs:d + cs + COL_CHUNK])
            u_s[0:2 * CONV_HALO, cs:cs + COL_CHUNK] = a * _sigmoid(b)
        starts_seq = sub == 0
        ends_seq = sub == N_SEG - 1
    else:
        starts_seq = sub % segs_per_seq == 0
        ends_seq = sub % segs_per_seq == segs_per_seq - 1

    def fill_halo(j):
        cols = slice(j * COL_W, (j + 1) * COL_W)
        if ext_halo:
            u_prev = jnp.where(i % tiles_per_seq == 0, 0.0, u_s[0:CONV_HALO, cols])
            u_next = jnp.where(i % tiles_per_seq == tiles_per_seq - 1, 0.0, u_s[CONV_HALO:2 * CONV_HALO, cols])
        front = u_s[back0 - halo_rows - 1:back0 - 1, cols]
        back = u_s[main0 + 1:main0 + 1 + halo_rows, cols]
        for p in range(CONV_HALO):
            rows = slice(p * N_SEG, (p + 1) * N_SEG)
            edge_f = jnp.broadcast_to(u_prev[p:p + 1, :], (N_SEG, COL_W)) if ext_halo else 0.0
            edge_b = jnp.broadcast_to(u_next[p:p + 1, :], (N_SEG, COL_W)) if ext_halo else 0.0
            u_s[p * N_SEG:(p + 1) * N_SEG, cols] = jnp.where(starts_seq, edge_f, front[rows])
            u_s[back0 + p * N_SEG:back0 + (p + 1) * N_SEG, cols] = jnp.where(ends_seq, edge_b, back[rows])

    def conv_chunk(j, r):
        r0 = row0(r)
        for cb in range(j * COL_W // LANES, (j + 1) * COL_W // LANES):
            lanes = slice(cb * LANES, (cb + 1) * LANES)
            n_pos = ROW_CHUNK // N_SEG
            accs = [wb_s[CONV_K * N_SEG:(CONV_K + 1) * N_SEG, lanes]] * n_pos
            for k in range(CONV_K):
                wk = wb_s[k * N_SEG:(k + 1) * N_SEG, lanes]
                k0 = r0 + (k + CONV_HALO - CONV_K // 2) * N_SEG
                accs = [acc + u_s[pl.ds(k0 + p * N_SEG, N_SEG), lanes] * wk for p, acc in enumerate(accs)]
            v_s[pl.ds(r0, ROW_CHUNK), lanes] = jnp.concatenate(accs, axis=0)

    for cb in range(d // COL_CHUNK):
        cs = cb * COL_CHUNK
        hm = h_s[0:tile, :]
        a = _dot(hm, win_ref[:, cs:cs + COL_CHUNK])
        b = _dot(hm, win_ref[:, d + cs:d + cs + COL_CHUNK])
        u_s[main0:main0 + tile, cs:cs + COL_CHUNK] = a * _sigmoid(b)
        z = _dot(hm, win_ref[:, 2 * d + cs:2 * d + cs + COL_CHUNK])
        sz_s[:, cs:cs + COL_CHUNK] = _silu(z)
    for j in range(n_cc):
        fill_halo(j)

    def conv_rows(r, carry):
        for j in range(n_cc):
            conv_chunk(j, r)
        return carry

    def norm_gate(r):
        for part in range(ROW_CHUNK // NORM_ROWS):
            rows = pl.ds(row0(r) + part * NORM_ROWS, NORM_ROWS)
            acc = v_s[rows, :]
            mu = jnp.mean(acc, axis=-1, keepdims=True)
            dev = acc - mu
            var = jnp.mean(dev * dev, axis=-1, keepdims=True)
            y = dev * lax.rsqrt(var + EPS) * lng_ref[...] + lnb_ref[...]
            gt_s[rows, :] = (_silu(y) * sz_s[rows, :]).astype(BF16)

    def norm_rows(r, carry):
        norm_gate(r)
        return carry

    lax.fori_loop(0, n_rc, conv_rows, 0)
    lax.fori_loop(0, n_rc, norm_rows, 0, unroll=4)
    for cb in range(d // COL_CHUNK):
        cols = slice(cb * COL_CHUNK, (cb + 1) * COL_CHUNK)
        hn_s[:, cols] = _dot(unperm_ref[...], gt_s[:, cols]).astype(BF16)
    o_ref[...] = x_ref[...] + gate * _dot(hn_s[...], wout_ref[...])


def _mod_spec(row0, row_step, tiles_per_seq):
    return pl.BlockSpec((None, 3, D_MODEL), lambda i: (row0 + row_step * (i // tiles_per_seq), 0, 0))


def _resident(shape):
    return pl.BlockSpec(shape, lambda *_: (0,) * len(shape), pipeline_mode=pl.Buffered(1))


def _conv_layer(x, mod, row0, row_step, seq_len, g, w_in, dw_w, dw_b, ln_g, ln_b, w_out):
    n_tok, d = x.shape
    tile = CONV_TILE
    seg = SEG_LEN
    segs_per_seq = seq_len // seg
    ext_halo = segs_per_seq > N_SEG
    tiles_per_seq = max(segs_per_seq // N_SEG, 1)
    assert (segs_per_seq % N_SEG == 0) if ext_halo else (N_SEG % segs_per_seq == 0)
    hb = tile // CONV_HALO
    n_hblk = n_tok // CONV_HALO
    tok_spec = pl.BlockSpec((tile, d), lambda i: (i, 0))
    in_specs = [tok_spec]
    args = [x]
    if ext_halo:
        in_specs += [pl.BlockSpec((CONV_HALO, d), lambda i: (jnp.maximum(i * hb - 1, 0), 0)),
                     pl.BlockSpec((CONV_HALO, d), lambda i: (jnp.minimum((i + 1) * hb, n_hblk - 1), 0))]
        args += [x, x]
    in_specs += [_mod_spec(row0, row_step, tiles_per_seq),
                 _resident((1, d)), _resident((d, 3 * d)), _resident((CONV_K, d)),
                 _resident((1, d)), _resident((1, d)), _resident((1, d)), _resident((d, d)),
                 _resident((tile, tile)), _resident((tile, tile))]
    perm = _segment_permutation(tile)
    args += [mod, g, w_in, dw_w, dw_b, ln_g, ln_b, w_out, jnp.asarray(perm, BF16), jnp.asarray(perm.T, BF16)]
    h_rows = tile + 2 * CONV_HALO if ext_halo else tile
    return pl.pallas_call(
        functools.partial(_conv_body, tile=tile, ext_halo=ext_halo, tiles_per_seq=tiles_per_seq,
                          segs_per_seq=segs_per_seq),
        grid=(n_tok // tile,),
        in_specs=in_specs,
        out_specs=tok_spec,
        out_shape=jax.ShapeDtypeStruct((n_tok, d), F32),
        scratch_shapes=[pltpu.VMEM((h_rows, d), BF16),
                        pltpu.VMEM((tile + 2 * CONV_HALO * N_SEG, d), F32),
                        pltpu.VMEM((tile, d), F32),
                        pltpu.VMEM((tile, d), BF16),
                        pltpu.VMEM((tile, d), F32),
                        pltpu.VMEM(((CONV_K + 1) * N_SEG, d), F32),
                        pltpu.VMEM((tile, d), BF16)],
        compiler_params=_params(("arbitrary",)),
        name="conv_layer",
    )(*args)


CTX_SEQS = 2


def _head_masks():
    lane = lax.broadcasted_iota(jnp.int32, (1, 2 * HEAD_DIM), 1)
    first = lane < HEAD_DIM
    return first, jnp.logical_not(first)


def _na_ctx_body(*refs, final, n_prev, seq_len):
    x_ref, mod_ref, g_ref, win_ref, wout_ref = refs[:5]
    fg_ref = refs[5] if final else None
    o_ref, k_ref, v_ref, q_s, kb_s, vb_s, sz_s, att_s = refs[-8:]
    d = D_MODEL
    if n_prev:
        kprev_ref, vprev_ref = refs[-10:-8]
        k_ref[:, 0:n_prev] = kprev_ref[...]
        v_ref[:, 0:n_prev] = vprev_ref[...]
    x = x_ref[...]
    gs = g_ref[...] * (1.0 + mod_ref[1:2, :])
    h = _modulate(x, gs, mod_ref[0:1, :])
    for cb in range(d // COL_CHUNK):
        cs = cb * COL_CHUNK
        q_s[:, cs:cs + COL_CHUNK] = (_dot(h, win_ref[:, cs:cs + COL_CHUNK]) * (HEAD_DIM ** -0.5)).astype(BF16)
        k = _dot(h, win_ref[:, d + cs:d + cs + COL_CHUNK])
        v = _dot(h, win_ref[:, 2 * d + cs:2 * d + cs + COL_CHUNK])
        for s in range(CTX_SEQS):
            k_ref[s, n_prev, :, cs:cs + COL_CHUNK] = k[s * seq_len:(s + 1) * seq_len]
            v_ref[s, n_prev, :, cs:cs + COL_CHUNK] = v[s * seq_len:(s + 1) * seq_len]
        kb_s[:, cs:cs + COL_CHUNK] = k.astype(BF16)
        vb_s[:, cs:cs + COL_CHUNK] = v.astype(BF16)
        sz_s[:, cs:cs + COL_CHUNK] = _silu(_dot(h, win_ref[:, 3 * d + cs:3 * d + cs + COL_CHUNK]))

    masks = _head_masks()
    for s in range(CTX_SEQS):
        rows = slice(s * seq_len, (s + 1) * seq_len)
        for hp in range(N_HEADS // 2):
            ls = hp * 2 * HEAD_DIM
            q = q_s[rows, ls:ls + 2 * HEAD_DIM]
            kp = kb_s[rows, ls:ls + 2 * HEAD_DIM]
            vp = vb_s[rows, ls:ls + 2 * HEAD_DIM]
            outs = []
            for hh in range(2):
                sc = _dot_nt(jnp.where(masks[hh], q, jnp.zeros_like(q)), kp)
                p = jnp.exp(sc - jnp.max(sc, axis=-1, keepdims=True))
                l = jnp.sum(p, axis=-1, keepdims=True)
                outs.append(_dot(p.astype(BF16), vp) / l)
            att = jnp.where(masks[0], outs[0], outs[1])
            att_s[rows, ls:ls + 2 * HEAD_DIM] = (att * sz_s[rows, ls:ls + 2 * HEAD_DIM]).astype(BF16)

    y = x + mod_ref[2:3, :] * _dot(att_s[...], wout_ref[...])
    if final:
        y = _rms(y) * fg_ref[...]
    o_ref[...] = y


def _na_ctx_layer(x, mod, g, w_in, w_out, seq_len, final_g=None, prev=None):
    n_tok, d = x.shape
    n_seq = n_tok // seq_len
    rows = CTX_SEQS * seq_len
    n_prev = 0 if prev is None else prev[0].shape[1]
    const = lambda b: (0, 0)
    final = final_g is not None
    tok_spec = pl.BlockSpec((rows, d), lambda b: (b, 0))
    in_specs = [tok_spec,
                pl.BlockSpec((None, 3, d), lambda b: (0, 0, 0)),
                pl.BlockSpec((1, d), const),
                pl.BlockSpec((d, 4 * d), const),
                pl.BlockSpec((d, d), const)]
    args = [x, mod, g, w_in, w_out]
    if final:
        in_specs.append(pl.BlockSpec((1, d), const))
        args.append(final_g)
    if n_prev:
        in_specs += [pl.BlockSpec((CTX_SEQS, n_prev, seq_len, d), lambda b: (b, 0, 0, 0))] * 2
        args += list(prev)
    stack_spec = pl.BlockSpec((CTX_SEQS, n_prev + 1, seq_len, d), lambda b: (b, 0, 0, 0))
    stack_shape = jax.ShapeDtypeStruct((n_seq, n_prev + 1, seq_len, d), F32)
    return pl.pallas_call(
        functools.partial(_na_ctx_body, final=final, n_prev=n_prev, seq_len=seq_len),
        grid=(n_seq // CTX_SEQS,),
        in_specs=in_specs,
        out_specs=[tok_spec, stack_spec, stack_spec],
        out_shape=[jax.ShapeDtypeStruct((n_tok, d), F32), stack_shape, stack_shape],
        scratch_shapes=[pltpu.VMEM((rows, d), BF16),
                        pltpu.VMEM((rows, d), BF16),
                        pltpu.VMEM((rows, d), BF16),
                        pltpu.VMEM((rows, d), F32),
                        pltpu.VMEM((rows, d), BF16)],
        compiler_params=_params(("arbitrary",)),
        name="na_ctx_layer",
    )(*args)


def _na_proj_body(x_ref, mod_ref, g_ref, win_ref, q_ref, k_ref, v_ref, sz_ref):
    d = D_MODEL
    gs = g_ref[...] * (1.0 + mod_ref[1:2, :])
    h = _modulate(x_ref[...], gs, mod_ref[0:1, :])
    for cb in range(d // COL_CHUNK):
        cs = cb * COL_CHUNK
        q_ref[:, cs:cs + COL_CHUNK] = (_dot(h, win_ref[:, cs:cs + COL_CHUNK]) * (HEAD_DIM ** -0.5 * LOG2_E)).astype(BF16)
        k_ref[:, cs:cs + COL_CHUNK] = _dot(h, win_ref[:, d + cs:d + cs + COL_CHUNK]).astype(BF16)
        v_ref[:, cs:cs + COL_CHUNK] = _dot(h, win_ref[:, 2 * d + cs:2 * d + cs + COL_CHUNK]).astype(BF16)
        sz_ref[:, cs:cs + COL_CHUNK] = _silu(_dot(h, win_ref[:, 3 * d + cs:3 * d + cs + COL_CHUNK])).astype(BF16)


def _na_proj(x, mod, row0, seq_len, tile, g, w_in):
    n_tok, d = x.shape
    tiles_per_seq = seq_len // tile
    const = lambda i: (0, 0)
    tok_spec = pl.BlockSpec((tile, d), lambda i: (i, 0))
    out = jax.ShapeDtypeStruct((n_tok, d), BF16)
    return pl.pallas_call(
        _na_proj_body,
        grid=(n_tok // tile,),
        in_specs=[tok_spec,
                  _mod_spec(row0, 1, tiles_per_seq),
                  pl.BlockSpec((1, d), const),
                  pl.BlockSpec((d, 4 * d), const)],
        out_specs=[tok_spec] * 4,
        out_shape=[out] * 4,
        compiler_params=_params(("arbitrary",)),
        name="na_proj",
    )(x, mod, g, w_in)


def _block_rows(qb):
    ks = min(max(Q_ROWS * qb - ROW_WIN // 2, 0), GRID_ROWS - K_ROWS)
    rs = [min(max(Q_ROWS * qb + qr - ROW_WIN // 2, 0), GRID_ROWS - ROW_WIN) for qr in range(Q_ROWS)]
    return ks, rs


_CLASS_BLOCKS = (0, 2, N_QBLK - 1)


def _na_att_body(q_ref, k_ref, v_ref, sz_ref, kc_ref, vc_ref, t2_ref, o_ref, bias_s, kc_s, vc_s):
    masks = _head_masks()
    first = masks[0]
    kc_s[...] = kc_ref[...].astype(BF16)

    def with_ones(v, hh):
        return jnp.where(masks[hh], v, jnp.ones_like(v))

    vc = vc_ref[...].astype(BF16)
    for hh in range(2):
        vc_s[hh] = with_ones(vc, hh)

    def tile_rows(qb, qr, m):
        ks, rs = _block_rows(qb)
        ka = ks + 2 * m
        va = rs[qr] <= ka < rs[qr] + ROW_WIN
        vb = rs[qr] <= ka + 1 < rs[qr] + ROW_WIN
        return va, vb, ka - (Q_ROWS * qb + qr) + ROW_WIN

    @pl.when(pl.program_id(1) == 0)
    def _():
        neg = jnp.full((GRID_W, 2 * GRID_W), NEG_INF, F32)
        for cls, qb in enumerate(_CLASS_BLOCKS):
            for hh in range(2):
                for qr in range(Q_ROWS):
                    for m in range(K_ROWS // 2):
                        va, vb, e = tile_rows(qb, qr, m)
                        if not (va or vb):
                            continue
                        t = t2_ref[hh, e]
                        if not va:
                            t = jnp.where(first, neg, t)
                        if not vb:
                            t = jnp.where(first, t, neg)
                        bias_s[cls * 2 + hh, qr * GRID_W:(qr + 1) * GRID_W,
                               m * 2 * GRID_W:(m + 1) * 2 * GRID_W] = t

    def block(q0, k0, cls):
        q = q_ref[pl.ds(q0, Q_BLK), :]
        kl = k_ref[pl.ds(k0, K_BLK), :]
        vl = v_ref[pl.ds(k0, K_BLK), :]
        qb = _CLASS_BLOCKS[cls]
        outs = []
        for hh in range(2):
            qm = jnp.where(masks[hh], q, jnp.zeros_like(q))
            sl = _dot_nt(qm, kl)
            sc = _dot_nt(qm, kc_s[...])
            p_l, p_c = [], []
            for qr in range(Q_ROWS):
                rows = slice(qr * GRID_W, (qr + 1) * GRID_W)
                tiles = {}
                for m in range(K_ROWS // 2):
                    va, vb, _ = tile_rows(qb, qr, m)
                    if va or vb:
                        cols = slice(m * 2 * GRID_W, (m + 1) * 2 * GRID_W)
                        tiles[m] = sl[rows, cols] + bias_s[cls * 2 + hh, rows, cols]
                s_ctx = sc[rows, :]
                mx = jnp.max(s_ctx, axis=-1, keepdims=True)
                for t in tiles.values():
                    mx = jnp.maximum(mx, jnp.max(t, axis=-1, keepdims=True))
                zero = jnp.zeros((GRID_W, 2 * GRID_W), BF16)
                p_l.append(jnp.concatenate(
                    [jnp.exp2(tiles[m] - mx).astype(BF16) if m in tiles else zero for m in range(K_ROWS // 2)],
                    axis=1))
                p_c.append(jnp.exp2(s_ctx - mx).astype(BF16))
            o = (_dot(jnp.concatenate(p_l, axis=0), with_ones(vl, hh))
                 + _dot(jnp.concatenate(p_c, axis=0), vc_s[hh]))
            outs.append(o / pltpu.roll(o, HEAD_DIM, 1))
        att = jnp.where(first, outs[0], outs[1])
        o_ref[pl.ds(q0, Q_BLK), :] = (att * sz_ref[pl.ds(q0, Q_BLK), :].astype(F32)).astype(BF16)

    block(0, _block_rows(0)[0] * GRID_W, 0)
    block((N_QBLK - 1) * Q_BLK, _block_rows(N_QBLK - 1)[0] * GRID_W, 2)

    def interior(qb, carry):
        q0 = pl.multiple_of(qb * Q_BLK, Q_BLK)
        block(q0, pl.multiple_of(q0 - (ROW_WIN // 2) * GRID_W, Q_BLK), 1)
        return carry

    lax.fori_loop(1, N_QBLK - 1, interior, 0, unroll=6)


def _na_att(q, k, v, sz, cache_k, cache_v, t2, layer_j, n_batch):
    n_tok, d = q.shape
    seq_len = n_tok // n_batch
    past = cache_k.shape[2]
    pw = 2 * HEAD_DIM
    tok_spec = pl.BlockSpec((seq_len, pw), lambda hp, b: (b, hp))
    cache_spec = pl.BlockSpec((None, None, past, pw), lambda hp, b: (b, layer_j, 0, hp))
    return pl.pallas_call(
        _na_att_body,
        grid=(N_HEADS // 2, n_batch),
        in_specs=[tok_spec, tok_spec, tok_spec, tok_spec, cache_spec, cache_spec,
                  pl.BlockSpec((None, 2, 2 * ROW_WIN, GRID_W, pw), lambda hp, b: (layer_j, hp, 0, 0, 0))],
        out_specs=tok_spec,
        out_shape=jax.ShapeDtypeStruct((n_tok, d), BF16),
        scratch_shapes=[pltpu.VMEM((3 * 2, Q_BLK, K_BLK), F32),
                        pltpu.VMEM((past, pw), BF16),
                        pltpu.VMEM((2, past, pw), BF16)],
        compiler_params=_params(("arbitrary", "arbitrary")),
        name="na_attention",
    )(q, k, v, sz, cache_k, cache_v, t2)


def _na_out_body(*refs, final):
    if final:
        x_ref, att_ref, mod_ref, wout_ref, fg_ref, o_ref = refs
    else:
        x_ref, att_ref, mod_ref, wout_ref, o_ref = refs
    y = x_ref[...] + mod_ref[2:3, :] * _dot(att_ref[...], wout_ref[...])
    if final:
        y = _rms(y) * fg_ref[...]
    o_ref[...] = y


def _na_out(x, att, mod, row0, seq_len, tile, w_out, final_g=None):
    n_tok, d = x.shape
    tiles_per_seq = seq_len // tile
    const = lambda i: (0, 0)
    tok_spec = pl.BlockSpec((tile, d), lambda i: (i, 0))
    final = final_g is not None
    in_specs = [tok_spec, tok_spec,
                _mod_spec(row0, 1, tiles_per_seq),
                pl.BlockSpec((d, d), const)]
    args = [x, att, mod, w_out]
    if final:
        in_specs.append(pl.BlockSpec((1, d), const))
        args.append(final_g)
    return pl.pallas_call(
        functools.partial(_na_out_body, final=final),
        grid=(n_tok // tile,),
        in_specs=in_specs,
        out_specs=tok_spec,
        out_shape=jax.ShapeDtypeStruct((n_tok, d), F32),
        compiler_params=_params(("arbitrary",)),
        name="na_out",
    )(*args)


def _bias_tables(rpb):
    n_dr, n_dc = 2 * ROW_WIN - 1, 2 * COL_WIN - 1
    qc = np.arange(GRID_W)[:, None]
    lane = np.arange(2 * GRID_W)[None, :]
    kc, half = lane % GRID_W, lane // GRID_W
    start = np.clip(qc - COL_WIN // 2, 0, GRID_W - COL_WIN)
    in_win = (kc >= start) & (kc < start + COL_WIN)
    dc = kc - qc + COL_WIN - 1
    onehot = ((np.arange(n_dc)[:, None, None] == dc[None]) & in_win[None])[None] \
        & (np.arange(2)[:, None, None, None] == half[None, None])
    padded = jnp.pad(rpb, ((0, 0), (0, 0), (1, 1), (0, 0)))
    rows = jnp.stack([padded[:, :, :n_dr + 1], padded[:, :, 1:n_dr + 2]], axis=3)
    t = jnp.einsum("lhexj,xjqk->lheqk", rows, jnp.asarray(onehot, F32), precision=lax.Precision.HIGHEST)
    dr = np.arange(n_dr + 1)[:, None, None] - 1 + half[None]
    valid = (dr >= 0) & (dr < n_dr) & in_win[None]
    return jnp.where(jnp.asarray(valid), t * LOG2_E, NEG_INF)


NA_TILE = 1024


def kernel(x_prompt, x_sample, c, cache_k, cache_v, c_ctx, norm_g, ada_w, ada_b, conv_w_in, conv_dw_w,
           conv_dw_b, conv_ln_g, conv_ln_b, conv_w_out, na_w_in, na_rpb, na_w_out, final_g):
    n_ctx, ctx_len, d = x_prompt.shape
    n_dec, dec_len, _ = x_sample.shape
    n_na = na_w_in.shape[0]
    past = cache_k.shape[2]

    cvec = jnp.concatenate([c_ctx[None], c, jnp.zeros((MOD_ROWS - 1 - n_dec, d), F32)], axis=0)
    mod = _ada(cvec, ada_w, ada_b).reshape(DEPTH, MOD_ROWS, 3, d)

    xc = x_prompt.reshape(n_ctx * ctx_len, d)
    xs = x_sample.reshape(n_dec * dec_len, d)
    ck = cache_k.reshape(n_dec, n_na, past, d)
    cv = cache_v.reshape(n_dec, n_na, past, d)
    fg = final_g.reshape(1, d)
    bias_tables = _bias_tables(na_rpb)
    ctx_kv = None
    for i in range(DEPTH):
        j = i // 2
        g = norm_g[i].reshape(1, d)
        if i % 2 == 0:
            wi = conv_w_in[j].astype(BF16)
            wo = conv_w_out[j].astype(BF16)
            rest = (g, wi, conv_dw_w[j], conv_dw_b[j].reshape(1, d), conv_ln_g[j].reshape(1, d),
                    conv_ln_b[j].reshape(1, d), wo)
            xc = _conv_layer(xc, mod[i], 0, 0, ctx_len, *rest)
            xs = _conv_layer(xs, mod[i], 1, 1, dec_len, *rest)
        else:
            wi = na_w_in[j].astype(BF16)
            wo = na_w_out[j].astype(BF16)
            last = i == DEPTH - 1
            xc, *ctx_kv = _na_ctx_layer(xc, mod[i], g, wi, wo, ctx_len, final_g=fg if last else None, prev=ctx_kv)
            q, k, v, sz = _na_proj(xs, mod[i], 1, dec_len, NA_TILE, g, wi)
            att = _na_att(q, k, v, sz, ck, cv, bias_tables, j, n_dec)
            xs = _na_out(xs, att, mod[i], 1, dec_len, NA_TILE, wo, final_g=fg if last else None)

    y_prompt = xc.reshape(n_ctx, ctx_len, d)
    y_sample = xs.reshape(n_dec, dec_len, d)
    new_cache_k, new_cache_v = (t.reshape(n_ctx, n_na, ctx_len, N_HEADS, HEAD_DIM) for t in ctx_kv)
    return (y_prompt, y_sample, new_cache_k, new_cache_v)
```

```python
import functools

import numpy as np
import jax
import jax.numpy as jnp
from jax import lax
from jax.experimental import pallas as pl
from jax.experimental.pallas import tpu as pltpu

F32 = jnp.float32
BF16 = jnp.bfloat16

D_MODEL = 1024
LANES = 128
DEPTH = 4
N_HEADS = 16
HEAD_DIM = 64
GRID_W = 64
GRID_ROWS = 32
CONV_K = 31
CONV_HALO = 16
ROW_WIN = 8
COL_WIN = 16
EPS = 1e-6
NEG_INF = -1e30
LOG2_E = 1.4426950408889634
MOD_ROWS = 16

Q_ROWS = 4
Q_BLK = Q_ROWS * GRID_W
K_ROWS = 12
K_BLK = K_ROWS * GRID_W
N_QBLK = GRID_ROWS // Q_ROWS

VMEM_LIMIT_BYTES = 56 * 1024 * 1024


def _params(sem):
    return pltpu.CompilerParams(dimension_semantics=sem, vmem_limit_bytes=VMEM_LIMIT_BYTES)


def _sigmoid(x):
    return 1.0 / (1.0 + jnp.exp(-x))


def _silu(x):
    return x * _sigmoid(x)


def _rms(x):
    return x * lax.rsqrt(jnp.mean(x * x, axis=-1, keepdims=True) + EPS)


def _dot(a, b):
    return jnp.dot(a, b, preferred_element_type=F32)


def _dot_nt(a, b):
    return lax.dot_general(a, b, (((1,), (1,)), ((), ())), preferred_element_type=F32)


ADA_TILE = 768


def _ada_body(c_ref, w_ref, b_ref, o_ref):
    s = _silu(c_ref[...]).astype(BF16)
    o_ref[...] = _dot(s, w_ref[...].astype(BF16)) + b_ref[...]


def _ada(cvec, ada_w, ada_b):
    n_out = 3 * D_MODEL
    return pl.pallas_call(
        _ada_body,
        grid=(DEPTH, n_out // ADA_TILE),
        in_specs=[pl.BlockSpec((MOD_ROWS, D_MODEL), lambda l, n: (0, 0)),
                  pl.BlockSpec((None, D_MODEL, ADA_TILE), lambda l, n: (l, 0, n)),
                  pl.BlockSpec((None, 1, ADA_TILE), lambda l, n: (l, 0, n))],
        out_specs=pl.BlockSpec((None, MOD_ROWS, ADA_TILE), lambda l, n: (l, 0, n)),
        out_shape=jax.ShapeDtypeStruct((DEPTH, MOD_ROWS, n_out), F32),
        compiler_params=_params(("arbitrary", "arbitrary")),
        name="ada_params",
    )(cvec, ada_w, ada_b.reshape(DEPTH, 1, n_out))


def _modulate(x, gs, shift):
    return (_rms(x) * gs + shift).astype(BF16)


N_SEG = 8
CONV_TILE = 1024
SEG_LEN = CONV_TILE // N_SEG
PRE_ROWS = 64
ROW_CHUNK = 128
COL_W = 256
NORM_ROWS = 32
COL_CHUNK = 512


PERM_ROWS = 256
PERM_POS = PERM_ROWS // N_SEG


def _block_permutation():
    p = np.zeros((PERM_ROWS, PERM_ROWS), np.float32)
    dst = np.arange(PERM_ROWS)
    p[dst, (dst % N_SEG) * PERM_POS + dst // N_SEG] = 1.0
    return p


def _conv_body(*refs, tile, ext_halo, tiles_per_seq, segs_per_seq):
    x_ref, refs = refs[0], refs[1:]
    if ext_halo:
        (xp_ref, xn_ref, mod_ref, g_ref, win_ref, dww_ref, dwb_ref, lng_ref, lnb_ref, wout_ref, perm_ref, unperm_ref,
         o_ref, h_s, u_s, sz_s, gt_s, v_s, wb_s, hn_s) = refs
    else:
        (mod_ref, g_ref, win_ref, dww_ref, dwb_ref, lng_ref, lnb_ref, wout_ref, perm_ref, unperm_ref,
         o_ref, h_s, u_s, sz_s, gt_s, v_s, wb_s, hn_s) = refs
    d = D_MODEL
    halo_rows = CONV_HALO * N_SEG
    main0 = halo_rows
    back0 = main0 + tile
    shift = mod_ref[0:1, :]
    gate = mod_ref[2:3, :]
    gs = g_ref[...] * (1.0 + mod_ref[1:2, :])

    def modulate_rows(c, carry):
        rows = pl.ds(pl.multiple_of(c * PRE_ROWS, PRE_ROWS), PRE_ROWS)
        hn_s[rows, :] = _modulate(x_ref[rows, :], gs, shift)
        return carry

    lax.fori_loop(0, tile // PRE_ROWS, modulate_rows, 0, unroll=4)
    seg = tile // N_SEG

    def segment_rows(b, blk):
        return slice(b * seg + blk * PERM_POS, b * seg + (blk + 1) * PERM_POS)

    for blk in range(tile // PERM_ROWS):
        src = jnp.concatenate([hn_s[segment_rows(b, blk), :] for b in range(N_SEG)], axis=0)
        h_s[blk * PERM_ROWS:(blk + 1) * PERM_ROWS, :] = _dot(perm_ref[...], src).astype(BF16)
    n_rc = tile // ROW_CHUNK
    n_cc = d // COL_W
    for k in range(CONV_K):
        wb_s[k * N_SEG:(k + 1) * N_SEG, :] = jnp.broadcast_to(dww_ref[k:k + 1, :], (N_SEG, d))
    wb_s[CONV_K * N_SEG:(CONV_K + 1) * N_SEG, :] = jnp.broadcast_to(dwb_ref[...], (N_SEG, d))

    def row0(r):
        return r * ROW_CHUNK if isinstance(r, int) else pl.multiple_of(r * ROW_CHUNK, ROW_CHUNK)

    sub = lax.broadcasted_iota(jnp.int32, (N_SEG, 1), 0)
    if ext_halo:
        i = pl.program_id(0)
        h_s[tile:tile + CONV_HALO, :] = _modulate(xp_ref[...], gs, shift)
        h_s[tile + CONV_HALO:tile + 2 * CONV_HALO, :] = _modulate(xn_ref[...], gs, shift)
        he = h_s[tile:tile + 2 * CONV_HALO, :]
        for cb in range(d // COL_CHUNK):
            cs = cb * COL_CHUNK
            a = _dot(he, win_ref[:, cs:cs + COL_CHUNK])
            b = _dot(he, win_ref[:, d + cs:d + cs + COL_CHUNK])
            u_s[0:2 * CONV_HALO, cs:cs + COL_CHUNK] = a * _sigmoid(b)
        starts_seq = sub == 0
        ends_seq = sub == N_SEG - 1
    else:
        starts_seq = sub % segs_per_seq == 0
        ends_seq = sub % segs_per_seq == segs_per_seq - 1

    def fill_halo(j):
        cols = slice(j * COL_W, (j + 1) * COL_W)
        if ext_halo:
            u_prev = jnp.where(i % tiles_per_seq == 0, 0.0, u_s[0:CONV_HALO, cols])
            u_next = jnp.where(i % tiles_per_seq == tiles_per_seq - 1, 0.0, u_s[CONV_HALO:2 * CONV_HALO, cols])
        front = u_s[back0 - halo_rows - 1:back0 - 1, cols]
        back = u_s[main0 + 1:main0 + 1 + halo_rows, cols]
        for p in range(CONV_HALO):
            rows = slice(p * N_SEG, (p + 1) * N_SEG)
            edge_f = jnp.broadcast_to(u_prev[p:p + 1, :], (N_SEG, COL_W)) if ext_halo else 0.0
            edge_b = jnp.broadcast_to(u_next[p:p + 1, :], (N_SEG, COL_W)) if ext_halo else 0.0
            u_s[p * N_SEG:(p + 1) * N_SEG, cols] = jnp.where(starts_seq, edge_f, front[rows])
            u_s[back0 + p * N_SEG:back0 + (p + 1) * N_SEG, cols] = jnp.where(ends_seq, edge_b, back[rows])

    def conv_chunk(j, r):
        r0 = row0(r)
        for cb in range(j * COL_W // LANES, (j + 1) * COL_W // LANES):
            lanes = slice(cb * LANES, (cb + 1) * LANES)
            n_pos = ROW_CHUNK // N_SEG
            accs = [wb_s[CONV_K * N_SEG:(CONV_K + 1) * N_SEG, lanes]] * n_pos
            for k in range(CONV_K):
                wk = wb_s[k * N_SEG:(k + 1) * N_SEG, lanes]
                k0 = r0 + (k + CONV_HALO - CONV_K // 2) * N_SEG
                accs = [acc + u_s[pl.ds(k0 + p * N_SEG, N_SEG), lanes] * wk for p, acc in enumerate(accs)]
            v_s[pl.ds(r0, ROW_CHUNK), lanes] = jnp.concatenate(accs, axis=0)

    for cb in range(d // COL_CHUNK):
        cs = cb * COL_CHUNK
        hm = h_s[0:tile, :]
        a = _dot(hm, win_ref[:, cs:cs + COL_CHUNK])
        b = _dot(hm, win_ref[:, d + cs:d + cs + COL_CHUNK])
        u_s[main0:main0 + tile, cs:cs + COL_CHUNK] = a * _sigmoid(b)
        z = _dot(hm, win_ref[:, 2 * d + cs:2 * d + cs + COL_CHUNK])
        sz_s[:, cs:cs + COL_CHUNK] = _silu(z)
    for j in range(n_cc):
        fill_halo(j)

    def conv_rows(r, carry):
        for j in range(n_cc):
            conv_chunk(j, r)
        return carry

    def norm_gate(r):
        for part in range(ROW_CHUNK // NORM_ROWS):
            rows = pl.ds(row0(r) + part * NORM_ROWS, NORM_ROWS)
            acc = v_s[rows, :]
            mu = jnp.mean(acc, axis=-1, keepdims=True)
            dev = acc - mu
            var = jnp.mean(dev * dev, axis=-1, keepdims=True)
            y = dev * lax.rsqrt(var + EPS) * lng_ref[...] + lnb_ref[...]
            gt_s[rows, :] = (_silu(y) * sz_s[rows, :]).astype(BF16)

    def norm_rows(r, carry):
        norm_gate(r)
        return carry

    lax.fori_loop(0, n_rc, conv_rows, 0)
    lax.fori_loop(0, n_rc, norm_rows, 0, unroll=4)
    for blk in range(tile // PERM_ROWS):
        t = _dot(unperm_ref[...], gt_s[blk * PERM_ROWS:(blk + 1) * PERM_ROWS, :]).astype(BF16)
        for b in range(N_SEG):
            hn_s[segment_rows(b, blk), :] = t[b * PERM_POS:(b + 1) * PERM_POS]
    o_ref[...] = x_ref[...] + gate * _dot(hn_s[...], wout_ref[...])


def _mod_spec(row0, row_step, tiles_per_seq):
    return pl.BlockSpec((None, 3, D_MODEL), lambda i: (row0 + row_step * (i // tiles_per_seq), 0, 0))


def _resident(shape):
    return pl.BlockSpec(shape, lambda *_: (0,) * len(shape), pipeline_mode=pl.Buffered(1))


def _conv_layer(x, mod, row0, row_step, seq_len, g, w_in, dw_w, dw_b, ln_g, ln_b, w_out):
    n_tok, d = x.shape
    tile = CONV_TILE
    seg = SEG_LEN
    segs_per_seq = seq_len // seg
    ext_halo = segs_per_seq > N_SEG
    tiles_per_seq = max(segs_per_seq // N_SEG, 1)
    assert (segs_per_seq % N_SEG == 0) if ext_halo else (N_SEG % segs_per_seq == 0)
    hb = tile // CONV_HALO
    n_hblk = n_tok // CONV_HALO
    tok_spec = pl.BlockSpec((tile, d), lambda i: (i, 0))
    in_specs = [tok_spec]
    args = [x]
    if ext_halo:
        in_specs += [pl.BlockSpec((CONV_HALO, d), lambda i: (jnp.maximum(i * hb - 1, 0), 0)),
                     pl.BlockSpec((CONV_HALO, d), lambda i: (jnp.minimum((i + 1) * hb, n_hblk - 1), 0))]
        args += [x, x]
    in_specs += [_mod_spec(row0, row_step, tiles_per_seq),
                 _resident((1, d)), _resident((d, 3 * d)), _resident((CONV_K, d)),
                 _resident((1, d)), _resident((1, d)), _resident((1, d)), _resident((d, d)),
                 _resident((PERM_ROWS, PERM_ROWS)), _resident((PERM_ROWS, PERM_ROWS))]
    perm = _block_permutation()
    args += [mod, g, w_in, dw_w, dw_b, ln_g, ln_b, w_out, jnp.asarray(perm, BF16), jnp.asarray(perm.T, BF16)]
    h_rows = tile + 2 * CONV_HALO if ext_halo else tile
    return pl.pallas_call(
        functools.partial(_conv_body, tile=tile, ext_halo=ext_halo, tiles_per_seq=tiles_per_seq,
                          segs_per_seq=segs_per_seq),
        grid=(n_tok // tile,),
        in_specs=in_specs,
        out_specs=tok_spec,
        out_shape=jax.ShapeDtypeStruct((n_tok, d), F32),
        scratch_shapes=[pltpu.VMEM((h_rows, d), BF16),
                        pltpu.VMEM((tile + 2 * CONV_HALO * N_SEG, d), F32),
                        pltpu.VMEM((tile, d), F32),
                        pltpu.VMEM((tile, d), BF16),
                        pltpu.VMEM((tile, d), F32),
                        pltpu.VMEM(((CONV_K + 1) * N_SEG, d), F32),
                        pltpu.VMEM((tile, d), BF16)],
        compiler_params=_params(("arbitrary",)),
        name="conv_layer",
    )(*args)


CTX_SEQS = 2


def _head_masks():
    lane = lax.broadcasted_iota(jnp.int32, (1, 2 * HEAD_DIM), 1)
    first = lane < HEAD_DIM
    return first, jnp.logical_not(first)


def _na_ctx_body(*refs, final, n_prev, seq_len):
    x_ref, mod_ref, g_ref, win_ref, wout_ref = refs[:5]
    fg_ref = refs[5] if final else None
    o_ref, k_ref, v_ref, q_s, kb_s, vb_s, sz_s, att_s = refs[-8:]
    d = D_MODEL
    if n_prev:
        kprev_ref, vprev_ref = refs[-10:-8]
        k_ref[:, 0:n_prev] = kprev_ref[...]
        v_ref[:, 0:n_prev] = vprev_ref[...]
    x = x_ref[...]
    gs = g_ref[...] * (1.0 + mod_ref[1:2, :])
    h = _modulate(x, gs, mod_ref[0:1, :])
    for cb in range(d // COL_CHUNK):
        cs = cb * COL_CHUNK
        q_s[:, cs:cs + COL_CHUNK] = (_dot(h, win_ref[:, cs:cs + COL_CHUNK]) * (HEAD_DIM ** -0.5)).astype(BF16)
        k = _dot(h, win_ref[:, d + cs:d + cs + COL_CHUNK])
        v = _dot(h, win_ref[:, 2 * d + cs:2 * d + cs + COL_CHUNK])
        for s in range(CTX_SEQS):
            k_ref[s, n_prev, :, cs:cs + COL_CHUNK] = k[s * seq_len:(s + 1) * seq_len]
            v_ref[s, n_prev, :, cs:cs + COL_CHUNK] = v[s * seq_len:(s + 1) * seq_len]
        kb_s[:, cs:cs + COL_CHUNK] = k.astype(BF16)
        vb_s[:, cs:cs + COL_CHUNK] = v.astype(BF16)
        sz_s[:, cs:cs + COL_CHUNK] = _silu(_dot(h, win_ref[:, 3 * d + cs:3 * d + cs + COL_CHUNK]))

    masks = _head_masks()
    for s in range(CTX_SEQS):
        rows = slice(s * seq_len, (s + 1) * seq_len)
        for hp in range(N_HEADS // 2):
            ls = hp * 2 * HEAD_DIM
            q = q_s[rows, ls:ls + 2 * HEAD_DIM]
            kp = kb_s[rows, ls:ls + 2 * HEAD_DIM]
            vp = vb_s[rows, ls:ls + 2 * HEAD_DIM]
            outs = []
            for hh in range(2):
                sc = _dot_nt(jnp.where(masks[hh], q, jnp.zeros_like(q)), kp)
                p = jnp.exp(sc - jnp.max(sc, axis=-1, keepdims=True))
                l = jnp.sum(p, axis=-1, keepdims=True)
                outs.append(_dot(p.astype(BF16), vp) / l)
            att = jnp.where(masks[0], outs[0], outs[1])
            att_s[rows, ls:ls + 2 * HEAD_DIM] = (att * sz_s[rows, ls:ls + 2 * HEAD_DIM]).astype(BF16)

    y = x + mod_ref[2:3, :] * _dot(att_s[...], wout_ref[...])
    if final:
        y = _rms(y) * fg_ref[...]
    o_ref[...] = y


def _na_ctx_layer(x, mod, g, w_in, w_out, seq_len, final_g=None, prev=None):
    n_tok, d = x.shape
    n_seq = n_tok // seq_len
    rows = CTX_SEQS * seq_len
    n_prev = 0 if prev is None else prev[0].shape[1]
    const = lambda b: (0, 0)
    final = final_g is not None
    tok_spec = pl.BlockSpec((rows, d), lambda b: (b, 0))
    in_specs = [tok_spec,
                pl.BlockSpec((None, 3, d), lambda b: (0, 0, 0)),
                pl.BlockSpec((1, d), const),
                pl.BlockSpec((d, 4 * d), const),
                pl.BlockSpec((d, d), const)]
    args = [x, mod, g, w_in, w_out]
    if final:
        in_specs.append(pl.BlockSpec((1, d), const))
        args.append(final_g)
    if n_prev:
        in_specs += [pl.BlockSpec((CTX_SEQS, n_prev, seq_len, d), lambda b: (b, 0, 0, 0))] * 2
        args += list(prev)
    stack_spec = pl.BlockSpec((CTX_SEQS, n_prev + 1, seq_len, d), lambda b: (b, 0, 0, 0))
    stack_shape = jax.ShapeDtypeStruct((n_seq, n_prev + 1, seq_len, d), F32)
    return pl.pallas_call(
        functools.partial(_na_ctx_body, final=final, n_prev=n_prev, seq_len=seq_len),
        grid=(n_seq // CTX_SEQS,),
        in_specs=in_specs,
        out_specs=[tok_spec, stack_spec, stack_spec],
        out_shape=[jax.ShapeDtypeStruct((n_tok, d), F32), stack_shape, stack_shape],
        scratch_shapes=[pltpu.VMEM((rows, d), BF16),
                        pltpu.VMEM((rows, d), BF16),
                        pltpu.VMEM((rows, d), BF16),
                        pltpu.VMEM((rows, d), F32),
                        pltpu.VMEM((rows, d), BF16)],
        compiler_params=_params(("arbitrary",)),
        name="na_ctx_layer",
    )(*args)


def _na_proj_body(x_ref, mod_ref, g_ref, win_ref, q_ref, k_ref, v_ref, sz_ref):
    d = D_MODEL
    gs = g_ref[...] * (1.0 + mod_ref[1:2, :])
    h = _modulate(x_ref[...], gs, mod_ref[0:1, :])
    for cb in range(d // COL_CHUNK):
        cs = cb * COL_CHUNK
        q_ref[:, cs:cs + COL_CHUNK] = (_dot(h, win_ref[:, cs:cs + COL_CHUNK]) * (HEAD_DIM ** -0.5 * LOG2_E)).astype(BF16)
        k_ref[:, cs:cs + COL_CHUNK] = _dot(h, win_ref[:, d + cs:d + cs + COL_CHUNK]).astype(BF16)
        v_ref[:, cs:cs + COL_CHUNK] = _dot(h, win_ref[:, 2 * d + cs:2 * d + cs + COL_CHUNK]).astype(BF16)
        sz_ref[:, cs:cs + COL_CHUNK] = _silu(_dot(h, win_ref[:, 3 * d + cs:3 * d + cs + COL_CHUNK])).astype(BF16)


def _na_proj(x, mod, row0, seq_len, tile, g, w_in):
    n_tok, d = x.shape
    tiles_per_seq = seq_len // tile
    const = lambda i: (0, 0)
    tok_spec = pl.BlockSpec((tile, d), lambda i: (i, 0))
    out = jax.ShapeDtypeStruct((n_tok, d), BF16)
    return pl.pallas_call(
        _na_proj_body,
        grid=(n_tok // tile,),
        in_specs=[tok_spec,
                  _mod_spec(row0, 1, tiles_per_seq),
                  pl.BlockSpec((1, d), const),
                  pl.BlockSpec((d, 4 * d), const)],
        out_specs=[tok_spec] * 4,
        out_shape=[out] * 4,
        compiler_params=_params(("arbitrary",)),
        name="na_proj",
    )(x, mod, g, w_in)


def _block_rows(qb):
    rs = [min(max(Q_ROWS * qb + qr - ROW_WIN // 2, 0), GRID_ROWS - ROW_WIN) for qr in range(Q_ROWS)]
    ks = min(rs) // 2 * 2
    n_rows = -(-(max(rs) + ROW_WIN - ks) // 2) * 2
    assert n_rows <= K_ROWS and ks + n_rows <= GRID_ROWS
    return ks, n_rows, rs


_CLASS_BLOCKS = (0, 2, N_QBLK - 1)


def _na_att_body(q_ref, k_ref, v_ref, sz_ref, kc_ref, vc_ref, t2_ref, o_ref, bias_s, kc_s, vc_s):
    masks = _head_masks()
    first = masks[0]
    kc_s[...] = kc_ref[...].astype(BF16)

    def with_ones(v, hh):
        return jnp.where(masks[hh], v, jnp.ones_like(v))

    vc = vc_ref[...].astype(BF16)
    for hh in range(2):
        vc_s[hh] = with_ones(vc, hh)

    def tile_rows(qb, qr, m):
        ks, _, rs = _block_rows(qb)
        ka = ks + 2 * m
        va = rs[qr] <= ka < rs[qr] + ROW_WIN
        vb = rs[qr] <= ka + 1 < rs[qr] + ROW_WIN
        return va, vb, ka - (Q_ROWS * qb + qr) + ROW_WIN

    @pl.when(pl.program_id(1) == 0)
    def _():
        neg = jnp.full((GRID_W, 2 * GRID_W), NEG_INF, F32)
        for cls, qb in enumerate(_CLASS_BLOCKS):
            for hh in range(2):
                for qr in range(Q_ROWS):
                    for m in range(_block_rows(qb)[1] // 2):
                        va, vb, e = tile_rows(qb, qr, m)
                        if not (va or vb):
                            continue
                        t = t2_ref[hh, e]
                        if not va:
                            t = jnp.where(first, neg, t)
                        if not vb:
                            t = jnp.where(first, t, neg)
                        bias_s[cls * 2 + hh, qr * GRID_W:(qr + 1) * GRID_W,
                               m * 2 * GRID_W:(m + 1) * 2 * GRID_W] = t

    def block(q0, k0, cls):
        qb = _CLASS_BLOCKS[cls]
        n_pairs = _block_rows(qb)[1] // 2
        q = q_ref[pl.ds(q0, Q_BLK), :]
        kl = k_ref[pl.ds(k0, n_pairs * 2 * GRID_W), :]
        vl = v_ref[pl.ds(k0, n_pairs * 2 * GRID_W), :]
        outs = []
        for hh in range(2):
            qm = jnp.where(masks[hh], q, jnp.zeros_like(q))
            sl = _dot_nt(qm, kl)
            sc = _dot_nt(qm, kc_s[...])
            p_l, p_c = [], []
            for qr in range(Q_ROWS):
                rows = slice(qr * GRID_W, (qr + 1) * GRID_W)
                tiles = {}
                for m in range(n_pairs):
                    va, vb, _ = tile_rows(qb, qr, m)
                    if va or vb:
                        cols = slice(m * 2 * GRID_W, (m + 1) * 2 * GRID_W)
                        tiles[m] = sl[rows, cols] + bias_s[cls * 2 + hh, rows, cols]
                s_ctx = sc[rows, :]
                mx = jnp.max(s_ctx, axis=-1, keepdims=True)
                for t in tiles.values():
                    mx = jnp.maximum(mx, jnp.max(t, axis=-1, keepdims=True))
                zero = jnp.zeros((GRID_W, 2 * GRID_W), BF16)
                p_l.append(jnp.concatenate(
                    [jnp.exp2(tiles[m] - mx).astype(BF16) if m in tiles else zero for m in range(n_pairs)],
                    axis=1))
                p_c.append(jnp.exp2(s_ctx - mx).astype(BF16))
            o = (_dot(jnp.concatenate(p_l, axis=0), with_ones(vl, hh))
                 + _dot(jnp.concatenate(p_c, axis=0), vc_s[hh]))
            outs.append(o / pltpu.roll(o, HEAD_DIM, 1))
        att = jnp.where(first, outs[0], outs[1])
        o_ref[pl.ds(q0, Q_BLK), :] = (att * sz_ref[pl.ds(q0, Q_BLK), :].astype(F32)).astype(BF16)

    block(0, _block_rows(0)[0] * GRID_W, 0)
    block((N_QBLK - 1) * Q_BLK, _block_rows(N_QBLK - 1)[0] * GRID_W, 2)

    def interior(qb, carry):
        q0 = pl.multiple_of(qb * Q_BLK, Q_BLK)
        block(q0, pl.multiple_of(q0 - (ROW_WIN // 2) * GRID_W, Q_BLK), 1)
        return carry

    lax.fori_loop(1, N_QBLK - 1, interior, 0, unroll=6)


def _na_att(q, k, v, sz, cache_k, cache_v, t2, layer_j, n_batch):
    n_tok, d = q.shape
    seq_len = n_tok // n_batch
    past = cache_k.shape[2]
    pw = 2 * HEAD_DIM
    tok_spec = pl.BlockSpec((seq_len, pw), lambda hp, b: (b, hp))
    cache_spec = pl.BlockSpec((None, None, past, pw), lambda hp, b: (b, layer_j, 0, hp))
    return pl.pallas_call(
        _na_att_body,
        grid=(N_HEADS // 2, n_batch),
        in_specs=[tok_spec, tok_spec, tok_spec, tok_spec, cache_spec, cache_spec,
                  pl.BlockSpec((None, 2, 2 * ROW_WIN, GRID_W, pw), lambda hp, b: (layer_j, hp, 0, 0, 0))],
        out_specs=tok_spec,
        out_shape=jax.ShapeDtypeStruct((n_tok, d), BF16),
        scratch_shapes=[pltpu.VMEM((3 * 2, Q_BLK, K_BLK), F32),
                        pltpu.VMEM((past, pw), BF16),
                        pltpu.VMEM((2, past, pw), BF16)],
        compiler_params=_params(("arbitrary", "arbitrary")),
        name="na_attention",
    )(q, k, v, sz, cache_k, cache_v, t2)


def _na_out_body(*refs, final):
    if final:
        x_ref, att_ref, mod_ref, wout_ref, fg_ref, o_ref = refs
    else:
        x_ref, att_ref, mod_ref, wout_ref, o_ref = refs
    y = x_ref[...] + mod_ref[2:3, :] * _dot(att_ref[...], wout_ref[...])
    if final:
        y = _rms(y) * fg_ref[...]
    o_ref[...] = y


def _na_out(x, att, mod, row0, seq_len, tile, w_out, final_g=None):
    n_tok, d = x.shape
    tiles_per_seq = seq_len // tile
    const = lambda i: (0, 0)
    tok_spec = pl.BlockSpec((tile, d), lambda i: (i, 0))
    final = final_g is not None
    in_specs = [tok_spec, tok_spec,
                _mod_spec(row0, 1, tiles_per_seq),
                pl.BlockSpec((d, d), const)]
    args = [x, att, mod, w_out]
    if final:
        in_specs.append(pl.BlockSpec((1, d), const))
        args.append(final_g)
    return pl.pallas_call(
        functools.partial(_na_out_body, final=final),
        grid=(n_tok // tile,),
        in_specs=in_specs,
        out_specs=tok_spec,
        out_shape=jax.ShapeDtypeStruct((n_tok, d), F32),
        compiler_params=_params(("arbitrary",)),
        name="na_out",
    )(*args)


def _bias_tables(rpb):
    n_dr, n_dc = 2 * ROW_WIN - 1, 2 * COL_WIN - 1
    qc = np.arange(GRID_W)[:, None]
    lane = np.arange(2 * GRID_W)[None, :]
    kc, half = lane % GRID_W, lane // GRID_W
    start = np.clip(qc - COL_WIN // 2, 0, GRID_W - COL_WIN)
    in_win = (kc >= start) & (kc < start + COL_WIN)
    dc = kc - qc + COL_WIN - 1
    onehot = ((np.arange(n_dc)[:, None, None] == dc[None]) & in_win[None])[None] \
        & (np.arange(2)[:, None, None, None] == half[None, None])
    padded = jnp.pad(rpb, ((0, 0), (0, 0), (1, 1), (0, 0)))
    rows = jnp.stack([padded[:, :, :n_dr + 1], padded[:, :, 1:n_dr + 2]], axis=3)
    t = jnp.einsum("lhexj,xjqk->lheqk", rows, jnp.asarray(onehot, F32), precision=lax.Precision.HIGHEST)
    dr = np.arange(n_dr + 1)[:, None, None] - 1 + half[None]
    valid = (dr >= 0) & (dr < n_dr) & in_win[None]
    return jnp.where(jnp.asarray(valid), t * LOG2_E, NEG_INF)


NA_TILE = 1024


def kernel(x_prompt, x_sample, c, cache_k, cache_v, c_ctx, norm_g, ada_w, ada_b, conv_w_in, conv_dw_w,
           conv_dw_b, conv_ln_g, conv_ln_b, conv_w_out, na_w_in, na_rpb, na_w_out, final_g):
    n_ctx, ctx_len, d = x_prompt.shape
    n_dec, dec_len, _ = x_sample.shape
    n_na = na_w_in.shape[0]
    past = cache_k.shape[2]

    cvec = jnp.concatenate([c_ctx[None], c, jnp.zeros((MOD_ROWS - 1 - n_dec, d), F32)], axis=0)
    mod = _ada(cvec, ada_w, ada_b).reshape(DEPTH, MOD_ROWS, 3, d)

    xc = x_prompt.reshape(n_ctx * ctx_len, d)
    xs = x_sample.reshape(n_dec * dec_len, d)
    ck = cache_k.reshape(n_dec, n_na, past, d)
    cv = cache_v.reshape(n_dec, n_na, past, d)
    fg = final_g.reshape(1, d)
    bias_tables = _bias_tables(na_rpb)
    ctx_kv = None
    for i in range(DEPTH):
        j = i // 2
        g = norm_g[i].reshape(1, d)
        if i % 2 == 0:
            wi = conv_w_in[j].astype(BF16)
            wo = conv_w_out[j].astype(BF16)
            rest = (g, wi, conv_dw_w[j], conv_dw_b[j].reshape(1, d), conv_ln_g[j].reshape(1, d),
                    conv_ln_b[j].reshape(1, d), wo)
            xc = _conv_layer(xc, mod[i], 0, 0, ctx_len, *rest)
            xs = _conv_layer(xs, mod[i], 1, 1, dec_len, *rest)
        else:
            wi = na_w_in[j].astype(BF16)
            wo = na_w_out[j].astype(BF16)
            last = i == DEPTH - 1
            xc, *ctx_kv = _na_ctx_layer(xc, mod[i], g, wi, wo, ctx_len, final_g=fg if last else None, prev=ctx_kv)
            q, k, v, sz = _na_proj(xs, mod[i], 1, dec_len, NA_TILE, g, wi)
            att = _na_att(q, k, v, sz, ck, cv, bias_tables, j, n_dec)
            xs = _na_out(xs, att, mod[i], 1, dec_len, NA_TILE, wo, final_g=fg if last else None)

    y_prompt = xc.reshape(n_ctx, ctx_len, d)
    y_sample = xs.reshape(n_dec, dec_len, d)
    new_cache_k, new_cache_v = (t.reshape(n_ctx, n_na, ctx_len, N_HEADS, HEAD_DIM) for t in ctx_kv)
    return (y_prompt, y_sample, new_cache_k, new_cache_v)
```

```python
import functools

import numpy as np
import jax
import jax.numpy as jnp
from jax import lax
from jax.experimental import pallas as pl
from jax.experimental.pallas import tpu as pltpu

F32 = jnp.float32
BF16 = jnp.bfloat16

D_MODEL = 1024
LANES = 128
DEPTH = 4
N_HEADS = 16
HEAD_DIM = 64
GRID_W = 64
GRID_ROWS = 32
CONV_K = 31
CONV_HALO = 16
ROW_WIN = 8
COL_WIN = 16
EPS = 1e-6
NEG_INF = -1e30
LOG2_E = 1.4426950408889634
MOD_ROWS = 16

Q_ROWS = 4
Q_BLK = Q_ROWS * GRID_W
K_ROWS = 12
K_BLK = K_ROWS * GRID_W
N_QBLK = GRID_ROWS // Q_ROWS

VMEM_LIMIT_BYTES = 56 * 1024 * 1024


def _params(sem):
    return pltpu.CompilerParams(dimension_semantics=sem, vmem_limit_bytes=VMEM_LIMIT_BYTES)


def _sigmoid(x):
    return 1.0 / (1.0 + jnp.exp(-x))


def _silu(x):
    return x * _sigmoid(x)


def _rms(x):
    return x * lax.rsqrt(jnp.mean(x * x, axis=-1, keepdims=True) + EPS)


def _dot(a, b):
    return jnp.dot(a, b, preferred_element_type=F32)


def _dot_nt(a, b):
    return lax.dot_general(a, b, (((1,), (1,)), ((), ())), preferred_element_type=F32)


ADA_TILE = 768


def _ada_body(c_ref, w_ref, b_ref, o_ref):
    s = _silu(c_ref[...]).astype(BF16)
    o_ref[...] = _dot(s, w_ref[...].astype(BF16)) + b_ref[...]


def _ada(cvec, ada_w, ada_b):
    n_out = 3 * D_MODEL
    return pl.pallas_call(
        _ada_body,
        grid=(DEPTH, n_out // ADA_TILE),
        in_specs=[pl.BlockSpec((MOD_ROWS, D_MODEL), lambda l, n: (0, 0)),
                  pl.BlockSpec((None, D_MODEL, ADA_TILE), lambda l, n: (l, 0, n)),
                  pl.BlockSpec((None, 1, ADA_TILE), lambda l, n: (l, 0, n))],
        out_specs=pl.BlockSpec((None, MOD_ROWS, ADA_TILE), lambda l, n: (l, 0, n)),
        out_shape=jax.ShapeDtypeStruct((DEPTH, MOD_ROWS, n_out), F32),
        compiler_params=_params(("arbitrary", "arbitrary")),
        name="ada_params",
    )(cvec, ada_w, ada_b.reshape(DEPTH, 1, n_out))


def _modulate(x, gs, shift):
    return (_rms(x) * gs + shift).astype(BF16)


N_SEG = 8
CONV_TILE = 1024
SEG_LEN = CONV_TILE // N_SEG
PRE_ROWS = 64
ROW_CHUNK = 128
COL_W = 256
NORM_ROWS = 32
COL_CHUNK = 512


PERM_ROWS = 256
PERM_POS = PERM_ROWS // N_SEG


def _block_permutation():
    p = np.zeros((PERM_ROWS, PERM_ROWS), np.float32)
    dst = np.arange(PERM_ROWS)
    p[dst, (dst % N_SEG) * PERM_POS + dst // N_SEG] = 1.0
    return p


def _conv_body(*refs, tile, ext_halo, tiles_per_seq, segs_per_seq):
    x_ref, refs = refs[0], refs[1:]
    if ext_halo:
        (xp_ref, xn_ref, mod_ref, g_ref, win_ref, dww_ref, dwb_ref, lng_ref, lnb_ref, wout_ref, perm_ref, unperm_ref,
         o_ref, h_s, u_s, sz_s, gt_s, v_s, wb_s, hn_s) = refs
    else:
        (mod_ref, g_ref, win_ref, dww_ref, dwb_ref, lng_ref, lnb_ref, wout_ref, perm_ref, unperm_ref,
         o_ref, h_s, u_s, sz_s, gt_s, v_s, wb_s, hn_s) = refs
    d = D_MODEL
    halo_rows = CONV_HALO * N_SEG
    main0 = halo_rows
    back0 = main0 + tile
    shift = mod_ref[0:1, :]
    gate = mod_ref[2:3, :]
    gs = g_ref[...] * (1.0 + mod_ref[1:2, :])

    def modulate_rows(c, carry):
        rows = pl.ds(pl.multiple_of(c * PRE_ROWS, PRE_ROWS), PRE_ROWS)
        hn_s[rows, :] = _modulate(x_ref[rows, :], gs, shift)
        return carry

    lax.fori_loop(0, tile // PRE_ROWS, modulate_rows, 0, unroll=4)
    seg = tile // N_SEG

    def segment_rows(b, blk):
        return slice(b * seg + blk * PERM_POS, b * seg + (blk + 1) * PERM_POS)

    for blk in range(tile // PERM_ROWS):
        src = jnp.concatenate([hn_s[segment_rows(b, blk), :] for b in range(N_SEG)], axis=0)
        h_s[blk * PERM_ROWS:(blk + 1) * PERM_ROWS, :] = _dot(perm_ref[...], src).astype(BF16)
    n_rc = tile // ROW_CHUNK
    n_cc = d // COL_W
    for k in range(CONV_K):
        wb_s[k * N_SEG:(k + 1) * N_SEG, :] = jnp.broadcast_to(dww_ref[k:k + 1, :], (N_SEG, d))
    wb_s[CONV_K * N_SEG:(CONV_K + 1) * N_SEG, :] = jnp.broadcast_to(dwb_ref[...], (N_SEG, d))

    def row0(r):
        return r * ROW_CHUNK if isinstance(r, int) else pl.multiple_of(r * ROW_CHUNK, ROW_CHUNK)

    sub = lax.broadcasted_iota(jnp.int32, (N_SEG, 1), 0)
    if ext_halo:
        i = pl.program_id(0)
        h_s[tile:tile + CONV_HALO, :] = _modulate(xp_ref[...], gs, shift)
        h_s[tile + CONV_HALO:tile + 2 * CONV_HALO, :] = _modulate(xn_ref[...], gs, shift)
        he = h_s[tile:tile + 2 * CONV_HALO, :]
        for cb in range(d // COL_CHUNK):
            cs = cb * COL_CHUNK
            a = _dot(he, win_ref[:, cs:cs + COL_CHUNK])
            b = _dot(he, win_ref[:, d + cs:d + cs + COL_CHUNK])
            u_s[0:2 * CONV_HALO, cs:cs + COL_CHUNK] = a * _sigmoid(b)
        starts_seq = sub == 0
        ends_seq = sub == N_SEG - 1
    else:
        starts_seq = sub % segs_per_seq == 0
        ends_seq = sub % segs_per_seq == segs_per_seq - 1

    def fill_halo(j):
        cols = slice(j * COL_W, (j + 1) * COL_W)
        if ext_halo:
            u_prev = jnp.where(i % tiles_per_seq == 0, 0.0, u_s[0:CONV_HALO, cols])
            u_next = jnp.where(i % tiles_per_seq == tiles_per_seq - 1, 0.0, u_s[CONV_HALO:2 * CONV_HALO, cols])
        front = u_s[back0 - halo_rows - 1:back0 - 1, cols]
        back = u_s[main0 + 1:main0 + 1 + halo_rows, cols]
        for p in range(CONV_HALO):
            rows = slice(p * N_SEG, (p + 1) * N_SEG)
            edge_f = jnp.broadcast_to(u_prev[p:p + 1, :], (N_SEG, COL_W)) if ext_halo else 0.0
            edge_b = jnp.broadcast_to(u_next[p:p + 1, :], (N_SEG, COL_W)) if ext_halo else 0.0
            u_s[p * N_SEG:(p + 1) * N_SEG, cols] = jnp.where(starts_seq, edge_f, front[rows])
            u_s[back0 + p * N_SEG:back0 + (p + 1) * N_SEG, cols] = jnp.where(ends_seq, edge_b, back[rows])

    def conv_chunk(j, r):
        r0 = row0(r)
        for cb in range(j * COL_W // LANES, (j + 1) * COL_W // LANES):
            lanes = slice(cb * LANES, (cb + 1) * LANES)
            n_pos = ROW_CHUNK // N_SEG
            accs = [wb_s[CONV_K * N_SEG:(CONV_K + 1) * N_SEG, lanes]] * n_pos
            for k in range(CONV_K):
                wk = wb_s[k * N_SEG:(k + 1) * N_SEG, lanes]
                k0 = r0 + (k + CONV_HALO - CONV_K // 2) * N_SEG
                accs = [acc + u_s[pl.ds(k0 + p * N_SEG, N_SEG), lanes] * wk for p, acc in enumerate(accs)]
            v_s[pl.ds(r0, ROW_CHUNK), lanes] = jnp.concatenate(accs, axis=0)

    for cb in range(d // COL_CHUNK):
        cs = cb * COL_CHUNK
        hm = h_s[0:tile, :]
        a = _dot(hm, win_ref[:, cs:cs + COL_CHUNK])
        b = _dot(hm, win_ref[:, d + cs:d + cs + COL_CHUNK])
        u_s[main0:main0 + tile, cs:cs + COL_CHUNK] = a * _sigmoid(b)
        z = _dot(hm, win_ref[:, 2 * d + cs:2 * d + cs + COL_CHUNK])
        sz_s[:, cs:cs + COL_CHUNK] = _silu(z)
    for j in range(n_cc):
        fill_halo(j)

    def conv_rows(r, carry):
        for j in range(n_cc):
            conv_chunk(j, r)
        return carry

    def norm_gate(r):
        for part in range(ROW_CHUNK // NORM_ROWS):
            rows = pl.ds(row0(r) + part * NORM_ROWS, NORM_ROWS)
            acc = v_s[rows, :]
            mu = jnp.mean(acc, axis=-1, keepdims=True)
            dev = acc - mu
            var = jnp.mean(dev * dev, axis=-1, keepdims=True)
            y = dev * lax.rsqrt(var + EPS) * lng_ref[...] + lnb_ref[...]
            gt_s[rows, :] = (_silu(y) * sz_s[rows, :]).astype(BF16)

    def norm_rows(r, carry):
        norm_gate(r)
        return carry

    lax.fori_loop(0, n_rc, conv_rows, 0)
    lax.fori_loop(0, n_rc, norm_rows, 0, unroll=4)
    for blk in range(tile // PERM_ROWS):
        t = _dot(unperm_ref[...], gt_s[blk * PERM_ROWS:(blk + 1) * PERM_ROWS, :]).astype(BF16)
        for b in range(N_SEG):
            hn_s[segment_rows(b, blk), :] = t[b * PERM_POS:(b + 1) * PERM_POS]
    o_ref[...] = x_ref[...] + gate * _dot(hn_s[...], wout_ref[...])


def _mod_spec(row0, row_step, tiles_per_seq):
    return pl.BlockSpec((None, 3, D_MODEL), lambda i: (row0 + row_step * (i // tiles_per_seq), 0, 0))


def _resident(shape):
    return pl.BlockSpec(shape, lambda *_: (0,) * len(shape), pipeline_mode=pl.Buffered(1))


def _conv_layer(x, mod, row0, row_step, seq_len, g, w_in, dw_w, dw_b, ln_g, ln_b, w_out):
    n_tok, d = x.shape
    tile = CONV_TILE
    seg = SEG_LEN
    segs_per_seq = seq_len // seg
    ext_halo = segs_per_seq > N_SEG
    tiles_per_seq = max(segs_per_seq // N_SEG, 1)
    assert (segs_per_seq % N_SEG == 0) if ext_halo else (N_SEG % segs_per_seq == 0)
    hb = tile // CONV_HALO
    n_hblk = n_tok // CONV_HALO
    tok_spec = pl.BlockSpec((tile, d), lambda i: (i, 0))
    in_specs = [tok_spec]
    args = [x]
    if ext_halo:
        in_specs += [pl.BlockSpec((CONV_HALO, d), lambda i: (jnp.maximum(i * hb - 1, 0), 0)),
                     pl.BlockSpec((CONV_HALO, d), lambda i: (jnp.minimum((i + 1) * hb, n_hblk - 1), 0))]
        args += [x, x]
    in_specs += [_mod_spec(row0, row_step, tiles_per_seq),
                 _resident((1, d)), _resident((d, 3 * d)), _resident((CONV_K, d)),
                 _resident((1, d)), _resident((1, d)), _resident((1, d)), _resident((d, d)),
                 _resident((PERM_ROWS, PERM_ROWS)), _resident((PERM_ROWS, PERM_ROWS))]
    perm = _block_permutation()
    args += [mod, g, w_in, dw_w, dw_b, ln_g, ln_b, w_out, jnp.asarray(perm, BF16), jnp.asarray(perm.T, BF16)]
    h_rows = tile + 2 * CONV_HALO if ext_halo else tile
    return pl.pallas_call(
        functools.partial(_conv_body, tile=tile, ext_halo=ext_halo, tiles_per_seq=tiles_per_seq,
                          segs_per_seq=segs_per_seq),
        grid=(n_tok // tile,),
        in_specs=in_specs,
        out_specs=tok_spec,
        out_shape=jax.ShapeDtypeStruct((n_tok, d), F32),
        scratch_shapes=[pltpu.VMEM((h_rows, d), BF16),
                        pltpu.VMEM((tile + 2 * CONV_HALO * N_SEG, d), F32),
                        pltpu.VMEM((tile, d), F32),
                        pltpu.VMEM((tile, d), BF16),
                        pltpu.VMEM((tile, d), F32),
                        pltpu.VMEM(((CONV_K + 1) * N_SEG, d), F32),
                        pltpu.VMEM((tile, d), BF16)],
        compiler_params=_params(("arbitrary",)),
        name="conv_layer",
    )(*args)


CTX_SEQS = 2


def _head_masks():
    lane = lax.broadcasted_iota(jnp.int32, (1, 2 * HEAD_DIM), 1)
    first = lane < HEAD_DIM
    return first, jnp.logical_not(first)


def _na_ctx_body(*refs, final, n_prev, seq_len):
    x_ref, mod_ref, g_ref, win_ref, wout_ref = refs[:5]
    fg_ref = refs[5] if final else None
    o_ref, k_ref, v_ref, q_s, kb_s, vb_s, sz_s, att_s = refs[-8:]
    d = D_MODEL
    if n_prev:
        kprev_ref, vprev_ref = refs[-10:-8]
        k_ref[:, 0:n_prev] = kprev_ref[...]
        v_ref[:, 0:n_prev] = vprev_ref[...]
    x = x_ref[...]
    gs = g_ref[...] * (1.0 + mod_ref[1:2, :])
    h = _modulate(x, gs, mod_ref[0:1, :])
    for cb in range(d // COL_CHUNK):
        cs = cb * COL_CHUNK
        q_s[:, cs:cs + COL_CHUNK] = (_dot(h, win_ref[:, cs:cs + COL_CHUNK]) * (HEAD_DIM ** -0.5)).astype(BF16)
        k = _dot(h, win_ref[:, d + cs:d + cs + COL_CHUNK])
        v = _dot(h, win_ref[:, 2 * d + cs:2 * d + cs + COL_CHUNK])
        for s in range(CTX_SEQS):
            k_ref[s, n_prev, :, cs:cs + COL_CHUNK] = k[s * seq_len:(s + 1) * seq_len]
            v_ref[s, n_prev, :, cs:cs + COL_CHUNK] = v[s * seq_len:(s + 1) * seq_len]
        kb_s[:, cs:cs + COL_CHUNK] = k.astype(BF16)
        vb_s[:, cs:cs + COL_CHUNK] = v.astype(BF16)
        sz_s[:, cs:cs + COL_CHUNK] = _silu(_dot(h, win_ref[:, 3 * d + cs:3 * d + cs + COL_CHUNK]))

    masks = _head_masks()
    for s in range(CTX_SEQS):
        rows = slice(s * seq_len, (s + 1) * seq_len)
        for hp in range(N_HEADS // 2):
            ls = hp * 2 * HEAD_DIM
            q = q_s[rows, ls:ls + 2 * HEAD_DIM]
            kp = kb_s[rows, ls:ls + 2 * HEAD_DIM]
            vp = vb_s[rows, ls:ls + 2 * HEAD_DIM]
            outs = []
            for hh in range(2):
                sc = _dot_nt(jnp.where(masks[hh], q, jnp.zeros_like(q)), kp)
                p = jnp.exp(sc - jnp.max(sc, axis=-1, keepdims=True))
                l = jnp.sum(p, axis=-1, keepdims=True)
                outs.append(_dot(p.astype(BF16), vp) / l)
            att = jnp.where(masks[0], outs[0], outs[1])
            att_s[rows, ls:ls + 2 * HEAD_DIM] = (att * sz_s[rows, ls:ls + 2 * HEAD_DIM]).astype(BF16)

    y = x + mod_ref[2:3, :] * _dot(att_s[...], wout_ref[...])
    if final:
        y = _rms(y) * fg_ref[...]
    o_ref[...] = y


def _na_ctx_layer(x, mod, g, w_in, w_out, seq_len, final_g=None, prev=None):
    n_tok, d = x.shape
    n_seq = n_tok // seq_len
    rows = CTX_SEQS * seq_len
    n_prev = 0 if prev is None else prev[0].shape[1]
    const = lambda b: (0, 0)
    final = final_g is not None
    tok_spec = pl.BlockSpec((rows, d), lambda b: (b, 0))
    in_specs = [tok_spec,
                pl.BlockSpec((None, 3, d), lambda b: (0, 0, 0)),
                pl.BlockSpec((1, d), const),
                pl.BlockSpec((d, 4 * d), const),
                pl.BlockSpec((d, d), const)]
    args = [x, mod, g, w_in, w_out]
    if final:
        in_specs.append(pl.BlockSpec((1, d), const))
        args.append(final_g)
    if n_prev:
        in_specs += [pl.BlockSpec((CTX_SEQS, n_prev, seq_len, d), lambda b: (b, 0, 0, 0))] * 2
        args += list(prev)
    stack_spec = pl.BlockSpec((CTX_SEQS, n_prev + 1, seq_len, d), lambda b: (b, 0, 0, 0))
    stack_shape = jax.ShapeDtypeStruct((n_seq, n_prev + 1, seq_len, d), F32)
    return pl.pallas_call(
        functools.partial(_na_ctx_body, final=final, n_prev=n_prev, seq_len=seq_len),
        grid=(n_seq // CTX_SEQS,),
        in_specs=in_specs,
        out_specs=[tok_spec, stack_spec, stack_spec],
        out_shape=[jax.ShapeDtypeStruct((n_tok, d), F32), stack_shape, stack_shape],
        scratch_shapes=[pltpu.VMEM((rows, d), BF16),
                        pltpu.VMEM((rows, d), BF16),
                        pltpu.VMEM((rows, d), BF16),
                        pltpu.VMEM((rows, d), F32),
                        pltpu.VMEM((rows, d), BF16)],
        compiler_params=_params(("arbitrary",)),
        name="na_ctx_layer",
    )(*args)


def _na_proj_body(x_ref, mod_ref, g_ref, win_ref, q_ref, k_ref, v_ref, sz_ref):
    d = D_MODEL
    gs = g_ref[...] * (1.0 + mod_ref[1:2, :])
    h = _modulate(x_ref[...], gs, mod_ref[0:1, :])
    for cb in range(d // COL_CHUNK):
        cs = cb * COL_CHUNK
        q_ref[:, cs:cs + COL_CHUNK] = (_dot(h, win_ref[:, cs:cs + COL_CHUNK]) * (HEAD_DIM ** -0.5 * LOG2_E)).astype(BF16)
        k_ref[:, cs:cs + COL_CHUNK] = _dot(h, win_ref[:, d + cs:d + cs + COL_CHUNK]).astype(BF16)
        v_ref[:, cs:cs + COL_CHUNK] = _dot(h, win_ref[:, 2 * d + cs:2 * d + cs + COL_CHUNK]).astype(BF16)
        sz_ref[:, cs:cs + COL_CHUNK] = _silu(_dot(h, win_ref[:, 3 * d + cs:3 * d + cs + COL_CHUNK])).astype(BF16)


def _na_proj(x, mod, row0, seq_len, tile, g, w_in):
    n_tok, d = x.shape
    tiles_per_seq = seq_len // tile
    const = lambda i: (0, 0)
    tok_spec = pl.BlockSpec((tile, d), lambda i: (i, 0))
    out = jax.ShapeDtypeStruct((n_tok, d), BF16)
    return pl.pallas_call(
        _na_proj_body,
        grid=(n_tok // tile,),
        in_specs=[tok_spec,
                  _mod_spec(row0, 1, tiles_per_seq),
                  pl.BlockSpec((1, d), const),
                  pl.BlockSpec((d, 4 * d), const)],
        out_specs=[tok_spec] * 4,
        out_shape=[out] * 4,
        compiler_params=_params(("arbitrary",)),
        name="na_proj",
    )(x, mod, g, w_in)


def _block_rows(qb):
    rs = [min(max(Q_ROWS * qb + qr - ROW_WIN // 2, 0), GRID_ROWS - ROW_WIN) for qr in range(Q_ROWS)]
    ks = min(rs) // 2 * 2
    n_rows = -(-(max(rs) + ROW_WIN - ks) // 2) * 2
    assert n_rows <= K_ROWS and ks + n_rows <= GRID_ROWS
    return ks, n_rows, rs


_CLASS_BLOCKS = (0, 2, N_QBLK - 1)


def _na_att_body(q_ref, k_ref, v_ref, sz_ref, kc_ref, vc_ref, t2_ref, o_ref, bias_s, kc_s, vc_s):
    masks = _head_masks()
    first = masks[0]
    kc_s[...] = kc_ref[...].astype(BF16)

    def with_ones(v, hh):
        return jnp.where(masks[hh], v, jnp.ones_like(v))

    vc = vc_ref[...].astype(BF16)
    for hh in range(2):
        vc_s[hh] = with_ones(vc, hh)

    def tile_rows(qb, qr, m):
        ks, _, rs = _block_rows(qb)
        ka = ks + 2 * m
        va = rs[qr] <= ka < rs[qr] + ROW_WIN
        vb = rs[qr] <= ka + 1 < rs[qr] + ROW_WIN
        return va, vb, ka - (Q_ROWS * qb + qr) + ROW_WIN

    @pl.when(pl.program_id(1) == 0)
    def _():
        neg = jnp.full((GRID_W, 2 * GRID_W), NEG_INF, F32)
        for cls, qb in enumerate(_CLASS_BLOCKS):
            for hh in range(2):
                for qr in range(Q_ROWS):
                    for m in range(_block_rows(qb)[1] // 2):
                        va, vb, e = tile_rows(qb, qr, m)
                        if not (va or vb):
                            continue
                        t = t2_ref[hh, e]
                        if not va:
                            t = jnp.where(first, neg, t)
                        if not vb:
                            t = jnp.where(first, t, neg)
                        bias_s[cls * 2 + hh, qr * GRID_W:(qr + 1) * GRID_W,
                               m * 2 * GRID_W:(m + 1) * 2 * GRID_W] = t

    def block(q0, k0, cls):
        qb = _CLASS_BLOCKS[cls]
        n_pairs = _block_rows(qb)[1] // 2
        q = q_ref[pl.ds(q0, Q_BLK), :]
        kl = k_ref[pl.ds(k0, n_pairs * 2 * GRID_W), :]
        vl = v_ref[pl.ds(k0, n_pairs * 2 * GRID_W), :]
        outs = []
        qm = jnp.concatenate([jnp.where(m, q, jnp.zeros_like(q)) for m in masks], axis=0)
        sl_both = _dot_nt(qm, kl)
        sc_both = _dot_nt(qm, kc_s[...])
        for hh in range(2):
            sl = sl_both[hh * Q_BLK:(hh + 1) * Q_BLK]
            sc = sc_both[hh * Q_BLK:(hh + 1) * Q_BLK]
            p_l, p_c = [], []
            for qr in range(Q_ROWS):
                rows = slice(qr * GRID_W, (qr + 1) * GRID_W)
                tiles = {}
                for m in range(n_pairs):
                    va, vb, _ = tile_rows(qb, qr, m)
                    if va or vb:
                        cols = slice(m * 2 * GRID_W, (m + 1) * 2 * GRID_W)
                        tiles[m] = sl[rows, cols] + bias_s[cls * 2 + hh, rows, cols]
                s_ctx = sc[rows, :]
                mx = jnp.max(s_ctx, axis=-1, keepdims=True)
                for t in tiles.values():
                    mx = jnp.maximum(mx, jnp.max(t, axis=-1, keepdims=True))
                zero = jnp.zeros((GRID_W, 2 * GRID_W), BF16)
                p_l.append(jnp.concatenate(
                    [jnp.exp2(tiles[m] - mx).astype(BF16) if m in tiles else zero for m in range(n_pairs)],
                    axis=1))
                p_c.append(jnp.exp2(s_ctx - mx).astype(BF16))
            o = (_dot(jnp.concatenate(p_l, axis=0), with_ones(vl, hh))
                 + _dot(jnp.concatenate(p_c, axis=0), vc_s[hh]))
            outs.append(o / pltpu.roll(o, HEAD_DIM, 1))
        att = jnp.where(first, outs[0], outs[1])
        o_ref[pl.ds(q0, Q_BLK), :] = (att * sz_ref[pl.ds(q0, Q_BLK), :].astype(F32)).astype(BF16)

    block(0, _block_rows(0)[0] * GRID_W, 0)
    block((N_QBLK - 1) * Q_BLK, _block_rows(N_QBLK - 1)[0] * GRID_W, 2)

    def interior(qb, carry):
        q0 = pl.multiple_of(qb * Q_BLK, Q_BLK)
        block(q0, pl.multiple_of(q0 - (ROW_WIN // 2) * GRID_W, Q_BLK), 1)
        return carry

    lax.fori_loop(1, N_QBLK - 1, interior, 0, unroll=6)


def _na_att(q, k, v, sz, cache_k, cache_v, t2, layer_j, n_batch):
    n_tok, d = q.shape
    seq_len = n_tok // n_batch
    past = cache_k.shape[2]
    pw = 2 * HEAD_DIM
    tok_spec = pl.BlockSpec((seq_len, pw), lambda hp, b: (b, hp))
    cache_spec = pl.BlockSpec((None, None, past, pw), lambda hp, b: (b, layer_j, 0, hp))
    return pl.pallas_call(
        _na_att_body,
        grid=(N_HEADS // 2, n_batch),
        in_specs=[tok_spec, tok_spec, tok_spec, tok_spec, cache_spec, cache_spec,
                  pl.BlockSpec((None, 2, 2 * ROW_WIN, GRID_W, pw), lambda hp, b: (layer_j, hp, 0, 0, 0))],
        out_specs=tok_spec,
        out_shape=jax.ShapeDtypeStruct((n_tok, d), BF16),
        scratch_shapes=[pltpu.VMEM((3 * 2, Q_BLK, K_BLK), F32),
                        pltpu.VMEM((past, pw), BF16),
                        pltpu.VMEM((2, past, pw), BF16)],
        compiler_params=_params(("arbitrary", "arbitrary")),
        name="na_attention",
    )(q, k, v, sz, cache_k, cache_v, t2)


def _na_out_body(*refs, final):
    if final:
        x_ref, att_ref, mod_ref, wout_ref, fg_ref, o_ref = refs
    else:
        x_ref, att_ref, mod_ref, wout_ref, o_ref = refs
    y = x_ref[...] + mod_ref[2:3, :] * _dot(att_ref[...], wout_ref[...])
    if final:
        y = _rms(y) * fg_ref[...]
    o_ref[...] = y


def _na_out(x, att, mod, row0, seq_len, tile, w_out, final_g=None):
    n_tok, d = x.shape
    tiles_per_seq = seq_len // tile
    const = lambda i: (0, 0)
    tok_spec = pl.BlockSpec((tile, d), lambda i: (i, 0))
    final = final_g is not None
    in_specs = [tok_spec, tok_spec,
                _mod_spec(row0, 1, tiles_per_seq),
                pl.BlockSpec((d, d), const)]
    args = [x, att, mod, w_out]
    if final:
        in_specs.append(pl.BlockSpec((1, d), const))
        args.append(final_g)
    return pl.pallas_call(
        functools.partial(_na_out_body, final=final),
        grid=(n_tok // tile,),
        in_specs=in_specs,
        out_specs=tok_spec,
        out_shape=jax.ShapeDtypeStruct((n_tok, d), F32),
        compiler_params=_params(("arbitrary",)),
        name="na_out",
    )(*args)


def _bias_tables(rpb):
    n_dr, n_dc = 2 * ROW_WIN - 1, 2 * COL_WIN - 1
    qc = np.arange(GRID_W)[:, None]
    lane = np.arange(2 * GRID_W)[None, :]
    kc, half = lane % GRID_W, lane // GRID_W
    start = np.clip(qc - COL_WIN // 2, 0, GRID_W - COL_WIN)
    in_win = (kc >= start) & (kc < start + COL_WIN)
    dc = kc - qc + COL_WIN - 1
    onehot = ((np.arange(n_dc)[:, None, None] == dc[None]) & in_win[None])[None] \
        & (np.arange(2)[:, None, None, None] == half[None, None])
    padded = jnp.pad(rpb, ((0, 0), (0, 0), (1, 1), (0, 0)))
    rows = jnp.stack([padded[:, :, :n_dr + 1], padded[:, :, 1:n_dr + 2]], axis=3)
    t = jnp.einsum("lhexj,xjqk->lheqk", rows, jnp.asarray(onehot, F32), precision=lax.Precision.HIGHEST)
    dr = np.arange(n_dr + 1)[:, None, None] - 1 + half[None]
    valid = (dr >= 0) & (dr < n_dr) & in_win[None]
    return jnp.where(jnp.asarray(valid), t * LOG2_E, NEG_INF)


NA_TILE = 1024


def kernel(x_prompt, x_sample, c, cache_k, cache_v, c_ctx, norm_g, ada_w, ada_b, conv_w_in, conv_dw_w,
           conv_dw_b, conv_ln_g, conv_ln_b, conv_w_out, na_w_in, na_rpb, na_w_out, final_g):
    n_ctx, ctx_len, d = x_prompt.shape
    n_dec, dec_len, _ = x_sample.shape
    n_na = na_w_in.shape[0]
    past = cache_k.shape[2]

    cvec = jnp.concatenate([c_ctx[None], c, jnp.zeros((MOD_ROWS - 1 - n_dec, d), F32)], axis=0)
    mod = _ada(cvec, ada_w, ada_b).reshape(DEPTH, MOD_ROWS, 3, d)

    xc = x_prompt.reshape(n_ctx * ctx_len, d)
    xs = x_sample.reshape(n_dec * dec_len, d)
    ck = cache_k.reshape(n_dec, n_na, past, d)
    cv = cache_v.reshape(n_dec, n_na, past, d)
    fg = final_g.reshape(1, d)
    bias_tables = _bias_tables(na_rpb)
    ctx_kv = None
    for i in range(DEPTH):
        j = i // 2
        g = norm_g[i].reshape(1, d)
        if i % 2 == 0:
            wi = conv_w_in[j].astype(BF16)
            wo = conv_w_out[j].astype(BF16)
            rest = (g, wi, conv_dw_w[j], conv_dw_b[j].reshape(1, d), conv_ln_g[j].reshape(1, d),
                    conv_ln_b[j].reshape(1, d), wo)
            xc = _conv_layer(xc, mod[i], 0, 0, ctx_len, *rest)
            xs = _conv_layer(xs, mod[i], 1, 1, dec_len, *rest)
        else:
            wi = na_w_in[j].astype(BF16)
            wo = na_w_out[j].astype(BF16)
            last = i == DEPTH - 1
            xc, *ctx_kv = _na_ctx_layer(xc, mod[i], g, wi, wo, ctx_len, final_g=fg if last else None, prev=ctx_kv)
            q, k, v, sz = _na_proj(xs, mod[i], 1, dec_len, NA_TILE, g, wi)
            att = _na_att(q, k, v, sz, ck, cv, bias_tables, j, n_dec)
            xs = _na_out(xs, att, mod[i], 1, dec_len, NA_TILE, wo, final_g=fg if last else None)

    y_prompt = xc.reshape(n_ctx, ctx_len, d)
    y_sample = xs.reshape(n_dec, dec_len, d)
    new_cache_k, new_cache_v = (t.reshape(n_ctx, n_na, ctx_len, N_HEADS, HEAD_DIM) for t in ctx_kv)
    return (y_prompt, y_sample, new_cache_k, new_cache_v)
```

```python
import functools

import numpy as np
import jax
import jax.numpy as jnp
from jax import lax
from jax.experimental import pallas as pl
from jax.experimental.pallas import tpu as pltpu

F32 = jnp.float32
BF16 = jnp.bfloat16

D_MODEL = 1024
LANES = 128
DEPTH = 4
N_HEADS = 16
HEAD_DIM = 64
GRID_W = 64
GRID_ROWS = 32
CONV_K = 31
CONV_HALO = 16
ROW_WIN = 8
COL_WIN = 16
EPS = 1e-6
NEG_INF = -1e30
LOG2_E = 1.4426950408889634
MOD_ROWS = 16

Q_ROWS = 4
Q_BLK = Q_ROWS * GRID_W
K_ROWS = 12
K_BLK = K_ROWS * GRID_W
N_QBLK = GRID_ROWS // Q_ROWS

VMEM_LIMIT_BYTES = 56 * 1024 * 1024


def _params(sem):
    return pltpu.CompilerParams(dimension_semantics=sem, vmem_limit_bytes=VMEM_LIMIT_BYTES)


def _sigmoid(x):
    return 1.0 / (1.0 + jnp.exp(-x))


def _silu(x):
    return x * _sigmoid(x)


def _rms(x):
    return x * lax.rsqrt(jnp.mean(x * x, axis=-1, keepdims=True) + EPS)


def _dot(a, b):
    return jnp.dot(a, b, preferred_element_type=F32)


def _dot_nt(a, b):
    return lax.dot_general(a, b, (((1,), (1,)), ((), ())), preferred_element_type=F32)


ADA_TILE = 768


def _ada_body(c_ref, w_ref, b_ref, o_ref):
    s = _silu(c_ref[...]).astype(BF16)
    o_ref[...] = _dot(s, w_ref[...].astype(BF16)) + b_ref[...]


def _ada(cvec, ada_w, ada_b):
    n_out = 3 * D_MODEL
    return pl.pallas_call(
        _ada_body,
        grid=(DEPTH, n_out // ADA_TILE),
        in_specs=[pl.BlockSpec((MOD_ROWS, D_MODEL), lambda l, n: (0, 0)),
                  pl.BlockSpec((None, D_MODEL, ADA_TILE), lambda l, n: (l, 0, n)),
                  pl.BlockSpec((None, 1, ADA_TILE), lambda l, n: (l, 0, n))],
        out_specs=pl.BlockSpec((None, MOD_ROWS, ADA_TILE), lambda l, n: (l, 0, n)),
        out_shape=jax.ShapeDtypeStruct((DEPTH, MOD_ROWS, n_out), F32),
        compiler_params=_params(("arbitrary", "arbitrary")),
        name="ada_params",
    )(cvec, ada_w, ada_b.reshape(DEPTH, 1, n_out))


def _modulate(x, gs, shift):
    return (_rms(x) * gs + shift).astype(BF16)


N_SEG = 8
CONV_TILE = 1024
SEG_LEN = CONV_TILE // N_SEG
PRE_ROWS = 64
ROW_CHUNK = 128
COL_W = 256
NORM_ROWS = 32
COL_CHUNK = 512


PERM_ROWS = 256
PERM_POS = PERM_ROWS // N_SEG


def _block_permutation():
    p = np.zeros((PERM_ROWS, PERM_ROWS), np.float32)
    dst = np.arange(PERM_ROWS)
    p[dst, (dst % N_SEG) * PERM_POS + dst // N_SEG] = 1.0
    return p


def _conv_body(*refs, tile, ext_halo, tiles_per_seq, segs_per_seq):
    x_ref, refs = refs[0], refs[1:]
    if ext_halo:
        (xp_ref, xn_ref, mod_ref, g_ref, win_ref, dww_ref, dwb_ref, lng_ref, lnb_ref, wout_ref, perm_ref, unperm_ref,
         o_ref, h_s, u_s, sz_s, gt_s, v_s, wb_s, hn_s) = refs
    else:
        (mod_ref, g_ref, win_ref, dww_ref, dwb_ref, lng_ref, lnb_ref, wout_ref, perm_ref, unperm_ref,
         o_ref, h_s, u_s, sz_s, gt_s, v_s, wb_s, hn_s) = refs
    d = D_MODEL
    halo_rows = CONV_HALO * N_SEG
    main0 = halo_rows
    back0 = main0 + tile
    shift = mod_ref[0:1, :]
    gate = mod_ref[2:3, :]
    gs = g_ref[...] * (1.0 + mod_ref[1:2, :])

    def modulate_rows(c, carry):
        rows = pl.ds(pl.multiple_of(c * PRE_ROWS, PRE_ROWS), PRE_ROWS)
        hn_s[rows, :] = _modulate(x_ref[rows, :], gs, shift)
        return carry

    lax.fori_loop(0, tile // PRE_ROWS, modulate_rows, 0, unroll=4)
    seg = tile // N_SEG

    def segment_rows(b, blk):
        return slice(b * seg + blk * PERM_POS, b * seg + (blk + 1) * PERM_POS)

    for blk in range(tile // PERM_ROWS):
        src = jnp.concatenate([hn_s[segment_rows(b, blk), :] for b in range(N_SEG)], axis=0)
        h_s[blk * PERM_ROWS:(blk + 1) * PERM_ROWS, :] = _dot(perm_ref[...], src).astype(BF16)
    n_rc = tile // ROW_CHUNK
    n_cc = d // COL_W
    for k in range(CONV_K):
        wb_s[k * N_SEG:(k + 1) * N_SEG, :] = jnp.broadcast_to(dww_ref[k:k + 1, :], (N_SEG, d))
    wb_s[CONV_K * N_SEG:(CONV_K + 1) * N_SEG, :] = jnp.broadcast_to(dwb_ref[...], (N_SEG, d))

    def row0(r):
        return r * ROW_CHUNK if isinstance(r, int) else pl.multiple_of(r * ROW_CHUNK, ROW_CHUNK)

    sub = lax.broadcasted_iota(jnp.int32, (N_SEG, 1), 0)
    if ext_halo:
        i = pl.program_id(0)
        h_s[tile:tile + CONV_HALO, :] = _modulate(xp_ref[...], gs, shift)
        h_s[tile + CONV_HALO:tile + 2 * CONV_HALO, :] = _modulate(xn_ref[...], gs, shift)
        he = h_s[tile:tile + 2 * CONV_HALO, :]
        for cb in range(d // COL_CHUNK):
            cs = cb * COL_CHUNK
            a = _dot(he, win_ref[:, cs:cs + COL_CHUNK])
            b = _dot(he, win_ref[:, d + cs:d + cs + COL_CHUNK])
            u_s[0:2 * CONV_HALO, cs:cs + COL_CHUNK] = a * _sigmoid(b)
        starts_seq = sub == 0
        ends_seq = sub == N_SEG - 1
    else:
        starts_seq = sub % segs_per_seq == 0
        ends_seq = sub % segs_per_seq == segs_per_seq - 1

    def fill_halo(j):
        cols = slice(j * COL_W, (j + 1) * COL_W)
        if ext_halo:
            u_prev = jnp.where(i % tiles_per_seq == 0, 0.0, u_s[0:CONV_HALO, cols])
            u_next = jnp.where(i % tiles_per_seq == tiles_per_seq - 1, 0.0, u_s[CONV_HALO:2 * CONV_HALO, cols])
        front = u_s[back0 - halo_rows - 1:back0 - 1, cols]
        back = u_s[main0 + 1:main0 + 1 + halo_rows, cols]
        for p in range(CONV_HALO):
            rows = slice(p * N_SEG, (p + 1) * N_SEG)
            edge_f = jnp.broadcast_to(u_prev[p:p + 1, :], (N_SEG, COL_W)) if ext_halo else 0.0
            edge_b = jnp.broadcast_to(u_next[p:p + 1, :], (N_SEG, COL_W)) if ext_halo else 0.0
            u_s[p * N_SEG:(p + 1) * N_SEG, cols] = jnp.where(starts_seq, edge_f, front[rows])
            u_s[back0 + p * N_SEG:back0 + (p + 1) * N_SEG, cols] = jnp.where(ends_seq, edge_b, back[rows])

    def conv_chunk(j, r):
        r0 = row0(r)
        for cb in range(j * COL_W // LANES, (j + 1) * COL_W // LANES):
            lanes = slice(cb * LANES, (cb + 1) * LANES)
            n_pos = ROW_CHUNK // N_SEG
            accs = [wb_s[CONV_K * N_SEG:(CONV_K + 1) * N_SEG, lanes]] * n_pos
            for k in range(CONV_K):
                wk = wb_s[k * N_SEG:(k + 1) * N_SEG, lanes]
                k0 = r0 + (k + CONV_HALO - CONV_K // 2) * N_SEG
                accs = [acc + u_s[pl.ds(k0 + p * N_SEG, N_SEG), lanes] * wk for p, acc in enumerate(accs)]
            v_s[pl.ds(r0, ROW_CHUNK), lanes] = jnp.concatenate(accs, axis=0)

    for cb in range(d // COL_CHUNK):
        cs = cb * COL_CHUNK
        hm = h_s[0:tile, :]
        a = _dot(hm, win_ref[:, cs:cs + COL_CHUNK])
        b = _dot(hm, win_ref[:, d + cs:d + cs + COL_CHUNK])
        u_s[main0:main0 + tile, cs:cs + COL_CHUNK] = a * _sigmoid(b)
        z = _dot(hm, win_ref[:, 2 * d + cs:2 * d + cs + COL_CHUNK])
        sz_s[:, cs:cs + COL_CHUNK] = _silu(z)
    for j in range(n_cc):
        fill_halo(j)

    def conv_rows(r, carry):
        for j in range(n_cc):
            conv_chunk(j, r)
        return carry

    def norm_gate(r):
        for part in range(ROW_CHUNK // NORM_ROWS):
            rows = pl.ds(row0(r) + part * NORM_ROWS, NORM_ROWS)
            acc = v_s[rows, :]
            mu = jnp.mean(acc, axis=-1, keepdims=True)
            dev = acc - mu
            var = jnp.mean(dev * dev, axis=-1, keepdims=True)
            y = dev * lax.rsqrt(var + EPS) * lng_ref[...] + lnb_ref[...]
            gt_s[rows, :] = (_silu(y) * sz_s[rows, :]).astype(BF16)

    def norm_rows(r, carry):
        norm_gate(r)
        return carry

    lax.fori_loop(0, n_rc, conv_rows, 0)
    lax.fori_loop(0, n_rc, norm_rows, 0, unroll=4)
    for blk in range(tile // PERM_ROWS):
        t = _dot(unperm_ref[...], gt_s[blk * PERM_ROWS:(blk + 1) * PERM_ROWS, :]).astype(BF16)
        for b in range(N_SEG):
            hn_s[segment_rows(b, blk), :] = t[b * PERM_POS:(b + 1) * PERM_POS]
    o_ref[...] = x_ref[...] + gate * _dot(hn_s[...], wout_ref[...])


def _mod_spec(row0, row_step, tiles_per_seq):
    return pl.BlockSpec((None, 3, D_MODEL), lambda i: (row0 + row_step * (i // tiles_per_seq), 0, 0))


def _resident(shape):
    return pl.BlockSpec(shape, lambda *_: (0,) * len(shape), pipeline_mode=pl.Buffered(1))


def _conv_layer(x, mod, row0, row_step, seq_len, g, w_in, dw_w, dw_b, ln_g, ln_b, w_out):
    n_tok, d = x.shape
    tile = CONV_TILE
    seg = SEG_LEN
    segs_per_seq = seq_len // seg
    ext_halo = segs_per_seq > N_SEG
    tiles_per_seq = max(segs_per_seq // N_SEG, 1)
    assert (segs_per_seq % N_SEG == 0) if ext_halo else (N_SEG % segs_per_seq == 0)
    hb = tile // CONV_HALO
    n_hblk = n_tok // CONV_HALO
    tok_spec = pl.BlockSpec((tile, d), lambda i: (i, 0))
    in_specs = [tok_spec]
    args = [x]
    if ext_halo:
        in_specs += [pl.BlockSpec((CONV_HALO, d), lambda i: (jnp.maximum(i * hb - 1, 0), 0)),
                     pl.BlockSpec((CONV_HALO, d), lambda i: (jnp.minimum((i + 1) * hb, n_hblk - 1), 0))]
        args += [x, x]
    in_specs += [_mod_spec(row0, row_step, tiles_per_seq),
                 _resident((1, d)), _resident((d, 3 * d)), _resident((CONV_K, d)),
                 _resident((1, d)), _resident((1, d)), _resident((1, d)), _resident((d, d)),
                 _resident((PERM_ROWS, PERM_ROWS)), _resident((PERM_ROWS, PERM_ROWS))]
    perm = _block_permutation()
    args += [mod, g, w_in, dw_w, dw_b, ln_g, ln_b, w_out, jnp.asarray(perm, BF16), jnp.asarray(perm.T, BF16)]
    h_rows = tile + 2 * CONV_HALO if ext_halo else tile
    return pl.pallas_call(
        functools.partial(_conv_body, tile=tile, ext_halo=ext_halo, tiles_per_seq=tiles_per_seq,
                          segs_per_seq=segs_per_seq),
        grid=(n_tok // tile,),
        in_specs=in_specs,
        out_specs=tok_spec,
        out_shape=jax.ShapeDtypeStruct((n_tok, d), F32),
        scratch_shapes=[pltpu.VMEM((h_rows, d), BF16),
                        pltpu.VMEM((tile + 2 * CONV_HALO * N_SEG, d), F32),
                        pltpu.VMEM((tile, d), F32),
                        pltpu.VMEM((tile, d), BF16),
                        pltpu.VMEM((tile, d), F32),
                        pltpu.VMEM(((CONV_K + 1) * N_SEG, d), F32),
                        pltpu.VMEM((tile, d), BF16)],
        compiler_params=_params(("arbitrary",)),
        name="conv_layer",
    )(*args)


CTX_SEQS = 2


def _head_masks():
    lane = lax.broadcasted_iota(jnp.int32, (1, 2 * HEAD_DIM), 1)
    first = lane < HEAD_DIM
    return first, jnp.logical_not(first)


def _na_ctx_body(*refs, final, n_prev, seq_len):
    x_ref, mod_ref, g_ref, win_ref, wout_ref = refs[:5]
    fg_ref = refs[5] if final else None
    o_ref, k_ref, v_ref, q_s, kb_s, vb_s, sz_s, att_s = refs[-8:]
    d = D_MODEL
    if n_prev:
        kprev_ref, vprev_ref = refs[-10:-8]
        k_ref[:, 0:n_prev] = kprev_ref[...]
        v_ref[:, 0:n_prev] = vprev_ref[...]
    x = x_ref[...]
    gs = g_ref[...] * (1.0 + mod_ref[1:2, :])
    h = _modulate(x, gs, mod_ref[0:1, :])
    for cb in range(d // COL_CHUNK):
        cs = cb * COL_CHUNK
        q_s[:, cs:cs + COL_CHUNK] = (_dot(h, win_ref[:, cs:cs + COL_CHUNK]) * (HEAD_DIM ** -0.5)).astype(BF16)
        k = _dot(h, win_ref[:, d + cs:d + cs + COL_CHUNK])
        v = _dot(h, win_ref[:, 2 * d + cs:2 * d + cs + COL_CHUNK])
        for s in range(CTX_SEQS):
            k_ref[s, n_prev, :, cs:cs + COL_CHUNK] = k[s * seq_len:(s + 1) * seq_len]
            v_ref[s, n_prev, :, cs:cs + COL_CHUNK] = v[s * seq_len:(s + 1) * seq_len]
        kb_s[:, cs:cs + COL_CHUNK] = k.astype(BF16)
        vb_s[:, cs:cs + COL_CHUNK] = v.astype(BF16)
        sz_s[:, cs:cs + COL_CHUNK] = _silu(_dot(h, win_ref[:, 3 * d + cs:3 * d + cs + COL_CHUNK]))

    masks = _head_masks()
    for s in range(CTX_SEQS):
        rows = slice(s * seq_len, (s + 1) * seq_len)
        for hp in range(N_HEADS // 2):
            ls = hp * 2 * HEAD_DIM
            q = q_s[rows, ls:ls + 2 * HEAD_DIM]
            kp = kb_s[rows, ls:ls + 2 * HEAD_DIM]
            vp = vb_s[rows, ls:ls + 2 * HEAD_DIM]
            outs = []
            for hh in range(2):
                sc = _dot_nt(jnp.where(masks[hh], q, jnp.zeros_like(q)), kp)
                p = jnp.exp(sc - jnp.max(sc, axis=-1, keepdims=True))
                l = jnp.sum(p, axis=-1, keepdims=True)
                outs.append(_dot(p.astype(BF16), vp) / l)
            att = jnp.where(masks[0], outs[0], outs[1])
            att_s[rows, ls:ls + 2 * HEAD_DIM] = (att * sz_s[rows, ls:ls + 2 * HEAD_DIM]).astype(BF16)

    y = x + mod_ref[2:3, :] * _dot(att_s[...], wout_ref[...])
    if final:
        y = _rms(y) * fg_ref[...]
    o_ref[...] = y


def _na_ctx_layer(x, mod, g, w_in, w_out, seq_len, final_g=None, prev=None):
    n_tok, d = x.shape
    n_seq = n_tok // seq_len
    rows = CTX_SEQS * seq_len
    n_prev = 0 if prev is None else prev[0].shape[1]
    const = lambda b: (0, 0)
    final = final_g is not None
    tok_spec = pl.BlockSpec((rows, d), lambda b: (b, 0))
    in_specs = [tok_spec,
                pl.BlockSpec((None, 3, d), lambda b: (0, 0, 0)),
                pl.BlockSpec((1, d), const),
                pl.BlockSpec((d, 4 * d), const),
                pl.BlockSpec((d, d), const)]
    args = [x, mod, g, w_in, w_out]
    if final:
        in_specs.append(pl.BlockSpec((1, d), const))
        args.append(final_g)
    if n_prev:
        in_specs += [pl.BlockSpec((CTX_SEQS, n_prev, seq_len, d), lambda b: (b, 0, 0, 0))] * 2
        args += list(prev)
    stack_spec = pl.BlockSpec((CTX_SEQS, n_prev + 1, seq_len, d), lambda b: (b, 0, 0, 0))
    stack_shape = jax.ShapeDtypeStruct((n_seq, n_prev + 1, seq_len, d), F32)
    return pl.pallas_call(
        functools.partial(_na_ctx_body, final=final, n_prev=n_prev, seq_len=seq_len),
        grid=(n_seq // CTX_SEQS,),
        in_specs=in_specs,
        out_specs=[tok_spec, stack_spec, stack_spec],
        out_shape=[jax.ShapeDtypeStruct((n_tok, d), F32), stack_shape, stack_shape],
        scratch_shapes=[pltpu.VMEM((rows, d), BF16),
                        pltpu.VMEM((rows, d), BF16),
                        pltpu.VMEM((rows, d), BF16),
                        pltpu.VMEM((rows, d), F32),
                        pltpu.VMEM((rows, d), BF16)],
        compiler_params=_params(("arbitrary",)),
        name="na_ctx_layer",
    )(*args)


def _na_proj_body(x_ref, mod_ref, g_ref, win_ref, q_ref, k_ref, v_ref, sz_ref):
    d = D_MODEL
    gs = g_ref[...] * (1.0 + mod_ref[1:2, :])
    h = _modulate(x_ref[...], gs, mod_ref[0:1, :])
    for cb in range(d // COL_CHUNK):
        cs = cb * COL_CHUNK
        q_ref[:, cs:cs + COL_CHUNK] = (_dot(h, win_ref[:, cs:cs + COL_CHUNK]) * (HEAD_DIM ** -0.5 * LOG2_E)).astype(BF16)
        k_ref[:, cs:cs + COL_CHUNK] = _dot(h, win_ref[:, d + cs:d + cs + COL_CHUNK]).astype(BF16)
        v_ref[:, cs:cs + COL_CHUNK] = _dot(h, win_ref[:, 2 * d + cs:2 * d + cs + COL_CHUNK]).astype(BF16)
        sz_ref[:, cs:cs + COL_CHUNK] = _silu(_dot(h, win_ref[:, 3 * d + cs:3 * d + cs + COL_CHUNK])).astype(BF16)


def _na_proj(x, mod, row0, seq_len, tile, g, w_in):
    n_tok, d = x.shape
    tiles_per_seq = seq_len // tile
    const = lambda i: (0, 0)
    tok_spec = pl.BlockSpec((tile, d), lambda i: (i, 0))
    out = jax.ShapeDtypeStruct((n_tok, d), BF16)
    return pl.pallas_call(
        _na_proj_body,
        grid=(n_tok // tile,),
        in_specs=[tok_spec,
                  _mod_spec(row0, 1, tiles_per_seq),
                  pl.BlockSpec((1, d), const),
                  pl.BlockSpec((d, 4 * d), const)],
        out_specs=[tok_spec] * 4,
        out_shape=[out] * 4,
        compiler_params=_params(("arbitrary",)),
        name="na_proj",
    )(x, mod, g, w_in)


def _block_rows(qb):
    rs = [min(max(Q_ROWS * qb + qr - ROW_WIN // 2, 0), GRID_ROWS - ROW_WIN) for qr in range(Q_ROWS)]
    ks = min(rs) // 2 * 2
    n_rows = -(-(max(rs) + ROW_WIN - ks) // 2) * 2
    assert n_rows <= K_ROWS and ks + n_rows <= GRID_ROWS
    return ks, n_rows, rs


_CLASS_BLOCKS = (0, 2, N_QBLK - 1)


def _na_att_body(q_ref, k_ref, v_ref, sz_ref, kc_ref, vc_ref, t2_ref, o_ref, bias_s, kc_s, vc_s):
    masks = _head_masks()
    first = masks[0]
    kc_s[...] = kc_ref[...].astype(BF16)

    def with_ones(v):
        return jnp.concatenate([v, jnp.ones_like(v)], axis=1)

    vc_s[...] = with_ones(vc_ref[...].astype(BF16))

    def tile_rows(qb, qr, m):
        ks, _, rs = _block_rows(qb)
        ka = ks + 2 * m
        va = rs[qr] <= ka < rs[qr] + ROW_WIN
        vb = rs[qr] <= ka + 1 < rs[qr] + ROW_WIN
        return va, vb, ka - (Q_ROWS * qb + qr) + ROW_WIN

    @pl.when(pl.program_id(1) == 0)
    def _():
        neg = jnp.full((GRID_W, 2 * GRID_W), NEG_INF, F32)
        for cls, qb in enumerate(_CLASS_BLOCKS):
            for hh in range(2):
                for qr in range(Q_ROWS):
                    for m in range(_block_rows(qb)[1] // 2):
                        va, vb, e = tile_rows(qb, qr, m)
                        if not (va or vb):
                            continue
                        t = t2_ref[hh, e]
                        if not va:
                            t = jnp.where(first, neg, t)
                        if not vb:
                            t = jnp.where(first, t, neg)
                        bias_s[cls * 2 + hh, qr * GRID_W:(qr + 1) * GRID_W,
                               m * 2 * GRID_W:(m + 1) * 2 * GRID_W] = t

    def block(q0, k0, cls):
        qb = _CLASS_BLOCKS[cls]
        n_pairs = _block_rows(qb)[1] // 2
        n_keys = n_pairs * 2 * GRID_W
        q = q_ref[q0:q0 + Q_BLK, :]
        vl = with_ones(v_ref[k0:k0 + n_keys, :])
        p_l_both, p_c_both = [], []
        qm = jnp.concatenate([jnp.where(m, q, jnp.zeros_like(q)) for m in masks], axis=0)
        sl_both = _dot_nt(qm, k_ref[k0:k0 + n_keys, :])
        sc_both = _dot_nt(qm, kc_s[...])
        for hh in range(2):
            sl = sl_both[hh * Q_BLK:(hh + 1) * Q_BLK]
            sc = sc_both[hh * Q_BLK:(hh + 1) * Q_BLK]
            p_l, p_c = [], []
            for qr in range(Q_ROWS):
                rows = slice(qr * GRID_W, (qr + 1) * GRID_W)
                tiles = {}
                for m in range(n_pairs):
                    va, vb, _ = tile_rows(qb, qr, m)
                    if va or vb:
                        cols = slice(m * 2 * GRID_W, (m + 1) * 2 * GRID_W)
                        tiles[m] = sl[rows, cols] + bias_s[cls * 2 + hh, rows, cols]
                s_ctx = sc[rows, :]
                mx = jnp.max(s_ctx, axis=-1, keepdims=True)
                for t in tiles.values():
                    mx = jnp.maximum(mx, jnp.max(t, axis=-1, keepdims=True))
                zero = jnp.zeros((GRID_W, 2 * GRID_W), BF16)
                p_l.append(jnp.concatenate(
                    [jnp.exp2(tiles[m] - mx).astype(BF16) if m in tiles else zero for m in range(n_pairs)],
                    axis=1))
                p_c.append(jnp.exp2(s_ctx - mx).astype(BF16))
            p_l_both += p_l
            p_c_both += p_c
        o = _dot(jnp.concatenate(p_l_both, axis=0), vl) + _dot(jnp.concatenate(p_c_both, axis=0), vc_s[...])
        outs = [o[hh * Q_BLK:(hh + 1) * Q_BLK, 0:2 * HEAD_DIM] / o[hh * Q_BLK:(hh + 1) * Q_BLK, 2 * HEAD_DIM:]
                for hh in range(2)]
        att = jnp.where(first, outs[0], outs[1])
        o_ref[q0:q0 + Q_BLK, :] = (att * sz_ref[q0:q0 + Q_BLK, :].astype(F32)).astype(BF16)

    for qb in range(N_QBLK):
        cls = 0 if qb == 0 else (2 if qb == N_QBLK - 1 else 1)
        block(qb * Q_BLK, _block_rows(qb)[0] * GRID_W, cls)


def _na_att(q, k, v, sz, cache_k, cache_v, t2, layer_j, n_batch):
    n_tok, d = q.shape
    seq_len = n_tok // n_batch
    past = cache_k.shape[2]
    pw = 2 * HEAD_DIM
    tok_spec = pl.BlockSpec((seq_len, pw), lambda hp, b: (b, hp))
    cache_spec = pl.BlockSpec((None, None, past, pw), lambda hp, b: (b, layer_j, 0, hp))
    return pl.pallas_call(
        _na_att_body,
        grid=(N_HEADS // 2, n_batch),
        in_specs=[tok_spec, tok_spec, tok_spec, tok_spec, cache_spec, cache_spec,
                  pl.BlockSpec((None, 2, 2 * ROW_WIN, GRID_W, pw), lambda hp, b: (layer_j, hp, 0, 0, 0))],
        out_specs=tok_spec,
        out_shape=jax.ShapeDtypeStruct((n_tok, d), BF16),
        scratch_shapes=[pltpu.VMEM((3 * 2, Q_BLK, K_BLK), F32),
                        pltpu.VMEM((past, pw), BF16),
                        pltpu.VMEM((past, 2 * pw), BF16)],
        compiler_params=_params(("arbitrary", "arbitrary")),
        name="na_attention",
    )(q, k, v, sz, cache_k, cache_v, t2)


def _na_out_body(*refs, final):
    if final:
        x_ref, att_ref, mod_ref, wout_ref, fg_ref, o_ref = refs
    else:
        x_ref, att_ref, mod_ref, wout_ref, o_ref = refs
    y = x_ref[...] + mod_ref[2:3, :] * _dot(att_ref[...], wout_ref[...])
    if final:
        y = _rms(y) * fg_ref[...]
    o_ref[...] = y


def _na_out(x, att, mod, row0, seq_len, tile, w_out, final_g=None):
    n_tok, d = x.shape
    tiles_per_seq = seq_len // tile
    const = lambda i: (0, 0)
    tok_spec = pl.BlockSpec((tile, d), lambda i: (i, 0))
    final = final_g is not None
    in_specs = [tok_spec, tok_spec,
                _mod_spec(row0, 1, tiles_per_seq),
                pl.BlockSpec((d, d), const)]
    args = [x, att, mod, w_out]
    if final:
        in_specs.append(pl.BlockSpec((1, d), const))
        args.append(final_g)
    return pl.pallas_call(
        functools.partial(_na_out_body, final=final),
        grid=(n_tok // tile,),
        in_specs=in_specs,
        out_specs=tok_spec,
        out_shape=jax.ShapeDtypeStruct((n_tok, d), F32),
        compiler_params=_params(("arbitrary",)),
        name="na_out",
    )(*args)


def _bias_tables(rpb):
    n_dr, n_dc = 2 * ROW_WIN - 1, 2 * COL_WIN - 1
    qc = np.arange(GRID_W)[:, None]
    lane = np.arange(2 * GRID_W)[None, :]
    kc, half = lane % GRID_W, lane // GRID_W
    start = np.clip(qc - COL_WIN // 2, 0, GRID_W - COL_WIN)
    in_win = (kc >= start) & (kc < start + COL_WIN)
    dc = kc - qc + COL_WIN - 1
    onehot = ((np.arange(n_dc)[:, None, None] == dc[None]) & in_win[None])[None] \
        & (np.arange(2)[:, None, None, None] == half[None, None])
    padded = jnp.pad(rpb, ((0, 0), (0, 0), (1, 1), (0, 0)))
    rows = jnp.stack([padded[:, :, :n_dr + 1], padded[:, :, 1:n_dr + 2]], axis=3)
    t = jnp.einsum("lhexj,xjqk->lheqk", rows, jnp.asarray(onehot, F32), precision=lax.Precision.HIGHEST)
    dr = np.arange(n_dr + 1)[:, None, None] - 1 + half[None]
    valid = (dr >= 0) & (dr < n_dr) & in_win[None]
    return jnp.where(jnp.asarray(valid), t * LOG2_E, NEG_INF)


NA_TILE = 1024


def kernel(x_prompt, x_sample, c, cache_k, cache_v, c_ctx, norm_g, ada_w, ada_b, conv_w_in, conv_dw_w,
           conv_dw_b, conv_ln_g, conv_ln_b, conv_w_out, na_w_in, na_rpb, na_w_out, final_g):
    n_ctx, ctx_len, d = x_prompt.shape
    n_dec, dec_len, _ = x_sample.shape
    n_na = na_w_in.shape[0]
    past = cache_k.shape[2]

    cvec = jnp.concatenate([c_ctx[None], c, jnp.zeros((MOD_ROWS - 1 - n_dec, d), F32)], axis=0)
    mod = _ada(cvec, ada_w, ada_b).reshape(DEPTH, MOD_ROWS, 3, d)

    xc = x_prompt.reshape(n_ctx * ctx_len, d)
    xs = x_sample.reshape(n_dec * dec_len, d)
    ck = cache_k.reshape(n_dec, n_na, past, d)
    cv = cache_v.reshape(n_dec, n_na, past, d)
    fg = final_g.reshape(1, d)
    bias_tables = _bias_tables(na_rpb)
    ctx_kv = None
    for i in range(DEPTH):
        j = i // 2
        g = norm_g[i].reshape(1, d)
        if i % 2 == 0:
            wi = conv_w_in[j].astype(BF16)
            wo = conv_w_out[j].astype(BF16)
            rest = (g, wi, conv_dw_w[j], conv_dw_b[j].reshape(1, d), conv_ln_g[j].reshape(1, d),
                    conv_ln_b[j].reshape(1, d), wo)
            xc = _conv_layer(xc, mod[i], 0, 0, ctx_len, *rest)
            xs = _conv_layer(xs, mod[i], 1, 1, dec_len, *rest)
        else:
            wi = na_w_in[j].astype(BF16)
            wo = na_w_out[j].astype(BF16)
            last = i == DEPTH - 1
            xc, *ctx_kv = _na_ctx_layer(xc, mod[i], g, wi, wo, ctx_len, final_g=fg if last else None, prev=ctx_kv)
            q, k, v, sz = _na_proj(xs, mod[i], 1, dec_len, NA_TILE, g, wi)
            att = _na_att(q, k, v, sz, ck, cv, bias_tables, j, n_dec)
            xs = _na_out(xs, att, mod[i], 1, dec_len, NA_TILE, wo, final_g=fg if last else None)

    y_prompt = xc.reshape(n_ctx, ctx_len, d)
    y_sample = xs.reshape(n_dec, dec_len, d)
    new_cache_k, new_cache_v = (t.reshape(n_ctx, n_na, ctx_len, N_HEADS, HEAD_DIM) for t in ctx_kv)
    return (y_prompt, y_sample, new_cache_k, new_cache_v)
```
